```python
import math
import jax
import jax.numpy as jnp
from jax import lax
import numpy as np

D_MODEL = 1024
BATCH = 8
SEQ = 2048
DEPTH = 1

CHUNK = 64
N_META = 16
D_MIX = 2 * D_MODEL
SB_WIDTH = D_MIX // 2
SB_HEAD_DIM = 64
SB_HEADS = SB_WIDTH // SB_HEAD_DIM
SB_QBLOCK = 128
SSD_WIDTH = D_MIX - SB_WIDTH
SSD_HEAD_DIM = 64
SSD_HEADS = SSD_WIDTH // SSD_HEAD_DIM
SSD_GROUPS = 2
SSD_HEADS_PER_GROUP = SSD_HEADS // SSD_GROUPS
SSD_STATE = 128
SSD_CONV = 4
SSD_XBC = SSD_WIDTH + 2 * SSD_GROUPS * SSD_STATE
IN_WIDTHS = (SB_WIDTH, SB_WIDTH, SB_WIDTH, SB_WIDTH, SSD_WIDTH, SSD_XBC, SSD_HEADS)
D_IN = sum(IN_WIDTHS)
EPS = 1e-5
DT_MIN = 1e-3
DT_MAX = 1e-1

kernel_name = "hybrid_stickbreak_ssd_block"


def rms_norm(x, w):
    xf = x.astype(jnp.float32)
    y = xf * lax.rsqrt(jnp.mean(xf * xf, axis=-1, keepdims=True) + EPS)
    return (y * w.astype(jnp.float32)).astype(x.dtype)


def split_cols(proj):
    outs = []
    start = 0
    for width in IN_WIDTHS:
        outs.append(proj[..., start:start + width])
        start += width
    return outs


def causal_depthwise_conv(x, w, b):
    c = x.shape[-1]
    y = lax.conv_general_dilated(
        x, w[:, None, :].astype(x.dtype), window_strides=(1,),
        padding=[(SSD_CONV - 1, 0)], dimension_numbers=("NWC", "WIO", "NWC"),
        feature_group_count=c)
    return y + b.astype(x.dtype)


def stick_breaking_attention(q, k, v):
    seqlen = q.shape[1]
    scale = 1.0 / math.sqrt(q.shape[-1])
    outs = []
    for q0 in range(0, seqlen, SB_QBLOCK):
        q1 = min(q0 + SB_QBLOCK, seqlen)
        qb = q[:, q0:q1]
        kb = k[:, :q1]
        vb = v[:, :q1]
        z = jnp.einsum("bqhd,bkhd->bhqk", qb, kb).astype(jnp.float32) * scale
        t_pos = jnp.arange(q0, q1)[:, None]
        s_pos = jnp.arange(q1)[None, :]
        strict = s_pos < t_pos
        log_keep = jnp.where(strict, jax.nn.log_sigmoid(-z), 0.0)
        after = lax.cumsum(log_keep, axis=3, reverse=True) - log_keep
        weights = jnp.where(strict, jnp.exp(jax.nn.log_sigmoid(z) + after), 0.0)
        outs.append(jnp.einsum("bhqk,bkhd->bqhd", weights.astype(v.dtype), vb))
    return jnp.concatenate(outs, axis=1)


def ssd_chunked(xs, dt, a, bmat, cmat, d_skip):
    f32 = jnp.float32
    bsz, seqlen, _ = xs.shape
    pad = (-seqlen) % CHUNK
    G, R, P, N = SSD_GROUPS, SSD_HEADS_PER_GROUP, SSD_HEAD_DIM, SSD_STATE
    x = xs.astype(f32).reshape(bsz, seqlen, SSD_HEADS, P)
    bm = bmat.astype(f32).reshape(bsz, seqlen, G, N)
    cm = cmat.astype(f32).reshape(bsz, seqlen, G, N)

    def front(t):
        return jnp.pad(t, ((0, 0), (pad, 0)) + ((0, 0),) * (t.ndim - 2))

    x, dt, bm, cm = front(x), front(dt), front(bm), front(cm)
    lp = seqlen + pad
    nc = lp // CHUNK
    x = x.reshape(bsz, nc, CHUNK, G, R, P)
    dt = dt.reshape(bsz, nc, CHUNK, G, R)
    bm = bm.reshape(bsz, nc, CHUNK, G, N)
    cm = cm.reshape(bsz, nc, CHUNK, G, N)

    xdt = x * dt[..., None]
    a_cum = jnp.cumsum(dt * a.reshape(G, R), axis=2)
    seg = a_cum[:, :, :, None] - a_cum[:, :, None]
    causal = jnp.tril(jnp.ones((CHUNK, CHUNK), bool))[:, :, None, None]
    decay_ls = jnp.exp(jnp.where(causal, seg, -jnp.inf))
    cb = jnp.einsum("bclgn,bcsgn->bclsg", cm, bm)
    y_diag = jnp.einsum("bclsg,bclsgr,bcsgrp->bclgrp", cb, decay_ls, xdt)
    decay_to_end = jnp.exp(a_cum[:, :, -1:] - a_cum)
    states = jnp.einsum("bclgn,bclgr,bclgrp->bcgrpn", bm, decay_to_end, xdt)
    chunk_decay = jnp.exp(a_cum[:, :, -1])

    def step(h, inp):
        st, dec = inp
        return h * dec[..., None, None] + st, h

    h0 = jnp.zeros((bsz, G, R, P, N), f32)
    _, h_start = lax.scan(step, h0, (jnp.moveaxis(states, 1, 0), jnp.moveaxis(chunk_decay, 1, 0)))
    h_start = jnp.moveaxis(h_start, 0, 1)
    y_off = jnp.einsum("bclgn,bcgrpn,bclgr->bclgrp", cm, h_start, jnp.exp(a_cum))
    y = y_diag + y_off + x * d_skip.astype(f32).reshape(G, R)[:, :, None]
    y = y.reshape(bsz, lp, SSD_WIDTH)[:, pad:]
    return y.astype(xs.dtype)


def hybrid_layer(h, norm_w, w_in, conv_w, conv_b, dt_bias, a_log, d_skip, sb_norm_w, ssd_norm_w, w_out):
    bsz, seqlen, _ = h.shape
    u = rms_norm(h, norm_w)
    proj = jnp.einsum("bld,de->ble", u, w_in)
    q, k, v, sb_gate, ssd_z, xbc, dt_raw = split_cols(proj)
    hs = (bsz, seqlen, SB_HEADS, SB_HEAD_DIM)
    o_sb = stick_breaking_attention(q.reshape(hs), k.reshape(hs), v.reshape(hs))
    o_sb = o_sb.reshape(bsz, seqlen, SB_WIDTH)
    y_sb = rms_norm(o_sb * jax.nn.silu(sb_gate), sb_norm_w)
    xbc = jax.nn.silu(causal_depthwise_conv(xbc, conv_w, conv_b))
    xs = xbc[..., :SSD_WIDTH]
    bmat = xbc[..., SSD_WIDTH:SSD_WIDTH + SSD_GROUPS * SSD_STATE]
    cmat = xbc[..., SSD_WIDTH + SSD_GROUPS * SSD_STATE:]
    dt = jax.nn.softplus(dt_raw.astype(jnp.float32) + dt_bias.astype(jnp.float32))
    a = -jnp.exp(a_log.astype(jnp.float32))
    o_ssd = ssd_chunked(xs, dt, a, bmat, cmat, d_skip)
    y_ssd = rms_norm(o_ssd * jax.nn.silu(ssd_z), ssd_norm_w)
    y = jnp.concatenate([y_sb, y_ssd], axis=-1)
    return h + jnp.einsum("ble,ed->bld", y, w_out)


def setup_inputs(seed: int = 0) -> dict:
    key = jax.random.key(seed)
    ks = jax.random.split(key, 13)
    nrm = jax.random.normal
    x = nrm(ks[0], (BATCH, SEQ, D_MODEL), jnp.float32)
    meta_tokens = nrm(ks[1], (N_META, D_MODEL), jnp.float32)
    norm_w = 1.0 + 0.02 * nrm(ks[2], (DEPTH, D_MODEL), jnp.float32)
    w_in = nrm(ks[3], (DEPTH, D_MODEL, D_IN), jnp.float32) * D_MODEL ** -0.5
    conv_w = nrm(ks[4], (DEPTH, SSD_CONV, SSD_XBC), jnp.float32) * SSD_CONV ** -0.5
    conv_b = 0.02 * nrm(ks[5], (DEPTH, SSD_XBC), jnp.float32)
    dt0 = jnp.exp(jax.random.uniform(ks[6], (DEPTH, SSD_HEADS), jnp.float32,
                                     minval=math.log(DT_MIN), maxval=math.log(DT_MAX)))
    dt_bias = dt0 + jnp.log(-jnp.expm1(-dt0))
    a_log = jnp.log(jax.random.uniform(ks[7], (DEPTH, SSD_HEADS), jnp.float32, minval=1.0, maxval=16.0))
    d_skip = 1.0 + 0.1 * nrm(ks[8], (DEPTH, SSD_HEADS), jnp.float32)
    sb_norm_w = 1.0 + 0.02 * nrm(ks[9], (DEPTH, SB_WIDTH), jnp.float32)
    ssd_norm_w = 1.0 + 0.02 * nrm(ks[10], (DEPTH, SSD_WIDTH), jnp.float32)
    w_out = nrm(ks[11], (DEPTH, D_MIX, D_MODEL), jnp.float32) * D_MIX ** -0.5
    final_norm_w = 1.0 + 0.02 * nrm(ks[12], (D_MODEL,), jnp.float32)
    return {"x": x, "meta_tokens": meta_tokens, "norm_w": norm_w, "w_in": w_in,
            "conv_w": conv_w, "conv_b": conv_b, "dt_bias": dt_bias, "a_log": a_log,
            "d_skip": d_skip, "sb_norm_w": sb_norm_w, "ssd_norm_w": ssd_norm_w,
            "w_out": w_out, "final_norm_w": final_norm_w}


def reference(x, meta_tokens, norm_w, w_in, conv_w, conv_b, dt_bias, a_log, d_skip,
              sb_norm_w, ssd_norm_w, w_out, final_norm_w):
    bsz = x.shape[0]
    meta = jnp.broadcast_to(meta_tokens.astype(x.dtype)[None], (bsz, N_META, D_MODEL))
    h = jnp.concatenate([meta, x], axis=1)
    for layer in range(DEPTH):
        h = hybrid_layer(h, norm_w[layer], w_in[layer], conv_w[layer], conv_b[layer],
                         dt_bias[layer], a_log[layer], d_skip[layer], sb_norm_w[layer],
                         ssd_norm_w[layer], w_out[layer])
    h = rms_norm(h, final_norm_w)
    return h[:, N_META:]
```

```python
import functools
import math

import jax
import jax.numpy as jnp
from jax import lax
from jax.experimental import pallas as pl
from jax.experimental.pallas import tpu as pltpu

F32 = jnp.float32
BF16 = jnp.bfloat16

D_MODEL = 1024
N_META = 16
HEAD_DIM = 64
SB_WIDTH = 1024
SSD_WIDTH = 1024
SSD_HEADS = 16
SSD_GROUPS = 2
SSD_STATE = 128
SSD_CONV = 4
SSD_XBC = SSD_WIDTH + 2 * SSD_GROUPS * SSD_STATE
D_MAIN = 4 * SB_WIDTH + SSD_WIDTH + SSD_XBC
EPS = 1e-5

LANES = 128
PAIR = 2 * HEAD_DIM
CHUNK = 64
TQ = 256
NEG_BIG = -1e30

COL_Q, COL_K, COL_V, COL_GATE, COL_Z, COL_XS = 0, 1, 2, 3, 4, 5
COL_B, COL_C = 24, 25


def _softplus(x):
    return jnp.maximum(x, 0.0) + jnp.log(1.0 + jnp.exp(-jnp.abs(x)))


def _silu(x):
    return x / (1.0 + jnp.exp(-x))


def _split3(x):
    hi = x.astype(BF16)
    r1 = x - hi.astype(F32)
    mid = r1.astype(BF16)
    lo = (r1 - mid.astype(F32)).astype(BF16)
    return hi, mid, lo


def _dot(a, b):
    return jnp.dot(a, b, preferred_element_type=F32)


def _dot_nt(a, b):
    return lax.dot_general(a, b, (((1,), (1,)), ((), ())), preferred_element_type=F32)


def _dot_tn(a, b):
    return lax.dot_general(a, b, (((0,), (0,)), ((), ())), preferred_element_type=F32)


def _inproj_kernel(x_ref, nw_ref, w_ref, wdt_ref, dtb_ref, proj_ref, dt_ref, u_ref):
    @pl.when(pl.program_id(1) == 0)
    def _():
        x = x_ref[...]
        u = x * lax.rsqrt(jnp.mean(x * x, axis=-1, keepdims=True) + EPS) * nw_ref[...]
        ub = u.astype(BF16)
        u_ref[...] = ub
        dt_ref[...] = _softplus(_dot(ub, wdt_ref[...]) + dtb_ref[...])

    proj_ref[...] = _dot(u_ref[...], w_ref[...]).astype(BF16)


def _inproj(x2d, norm_w, w_main, w_dt, dt_bias, *, tm, tn=512):
    rows = x2d.shape[0]
    return pl.pallas_call(
        _inproj_kernel,
        grid=(rows // tm, D_MAIN // tn),
        in_specs=[
            pl.BlockSpec((tm, D_MODEL), lambda i, j: (i, 0)),
            pl.BlockSpec((1, D_MODEL), lambda i, j: (0, 0)),
            pl.BlockSpec((D_MODEL, tn), lambda i, j: (0, j)),
            pl.BlockSpec((D_MODEL, LANES), lambda i, j: (0, 0)),
            pl.BlockSpec((1, LANES), lambda i, j: (0, 0)),
        ],
        out_specs=[
            pl.BlockSpec((tm, tn), lambda i, j: (i, j)),
            pl.BlockSpec((tm, LANES), lambda i, j: (i, 0)),
        ],
        out_shape=[
            jax.ShapeDtypeStruct((rows, D_MAIN), BF16),
            jax.ShapeDtypeStruct((rows, LANES), F32),
        ],
        scratch_shapes=[pltpu.VMEM((tm, D_MODEL), BF16)],
        compiler_params=pltpu.CompilerParams(dimension_semantics=("arbitrary", "arbitrary")),
        name="inproj",
    )(x2d, norm_w, w_main, w_dt, dt_bias)


def _ssd_kernel(z_ref, xs_ref, b_ref, c_ref, dt_ref, s0_ref, tail0_ref, cw_ref, cb_ref,
                alog_ref, dsk_ref, nw_ref, e_ref, *rest, ts, emit_state):
    if emit_state:
        y_ref, sfin_ref, tailfin_ref, s_scr, ext_scr, xbc_scr, y_scr = rest
    else:
        y_ref, s_scr, ext_scr, xbc_scr, y_scr = rest

    @pl.when(pl.program_id(1) == 0)
    def _():
        s_scr[...] = s0_ref[...]
        ext_scr[0:8, :] = tail0_ref[...]

    ext_scr[8:8 + ts, 0:SSD_WIDTH] = xs_ref[...].astype(F32)
    ext_scr[8:8 + ts, SSD_WIDTH:SSD_WIDTH + 256] = b_ref[...].astype(F32)
    ext_scr[8:8 + ts, SSD_WIDTH + 256:SSD_XBC] = c_ref[...].astype(F32)
    conv = cb_ref[...] + cw_ref[0:1, :] * ext_scr[5:5 + ts, :]
    for k in range(1, SSD_CONV):
        conv = conv + cw_ref[k:k + 1, :] * ext_scr[5 + k:5 + k + ts, :]
    xbc_scr[...] = _silu(conv)
    ext_scr[0:8, :] = ext_scr[ts:ts + 8, :]

    a_row = -jnp.exp(alog_ref[...])
    li = lax.broadcasted_iota(jnp.int32, (CHUNK, CHUNK), 0)
    si = lax.broadcasted_iota(jnp.int32, (CHUNK, CHUNK), 1)
    t_incl = (si <= li).astype(BF16)
    l2 = lax.broadcasted_iota(jnp.int32, (CHUNK, LANES), 0)
    lane2 = lax.broadcasted_iota(jnp.int32, (CHUNK, LANES), 1)
    mask2 = (lane2 % CHUNK) <= l2
    low_half = lane2 < HEAD_DIM
    e_mat = e_ref[...]

    for c in range(ts // CHUNK):
        r0 = c * CHUNK
        xs = xbc_scr[r0:r0 + CHUNK, 0:SSD_WIDTH]
        bm = xbc_scr[r0:r0 + CHUNK, SSD_WIDTH:SSD_WIDTH + 256]
        cm = xbc_scr[r0:r0 + CHUNK, SSD_WIDTH + 256:SSD_XBC]
        dt = dt_ref[r0:r0 + CHUNK, :]
        dta = dt * a_row
        h3 = _split3(dta)
        acum = _dot(t_incl, h3[0]) + _dot(t_incl, h3[1]) + _dot(t_incl, h3[2])
        st3 = _split3(jnp.concatenate([dt, acum], axis=0))
        rep = _dot(st3[0], e_mat) + _dot(st3[1], e_mat) + _dot(st3[2], e_mat)
        dtr = rep[0:CHUNK]
        acr = rep[CHUNK:2 * CHUNK]
        atot = acr[CHUNK - 1:CHUNK, :]
        wt = jnp.concatenate([acum, pltpu.roll(acum, LANES - 1, axis=1)], axis=0).T
        xdt = xs * dtr
        for g in range(SSD_GROUPS):
            gs = slice(g * 512, (g + 1) * 512)
            sg = s_scr[g]
            cg = cm[:, g * SSD_STATE:(g + 1) * SSD_STATE].astype(BF16)
            bg = bm[:, g * SSD_STATE:(g + 1) * SSD_STATE].astype(BF16)
            yoff = _dot(cg, sg.astype(BF16)) * jnp.exp(acr[:, gs])
            cb2 = _dot_nt(cg, jnp.concatenate([bg, bg], axis=0))
            for pp in range(4):
                p = 4 * g + pp
                cols = slice(p * PAIR, (p + 1) * PAIR)
                seg = acr[:, cols] - wt[2 * p:2 * p + 1, :]
                m2 = (cb2 * jnp.exp(jnp.where(mask2, seg, NEG_BIG))).astype(BF16)
                x2 = xdt[:, cols].astype(BF16)
                zero = jnp.zeros_like(x2)
                xblk = jnp.concatenate([jnp.where(low_half, x2, zero),
                                        jnp.where(low_half, zero, x2)], axis=0)
                y_scr[r0:r0 + CHUNK, cols] = (_dot(m2, xblk) + yoff[:, pp * PAIR:(pp + 1) * PAIR]
                                              + xs[:, cols] * dsk_ref[:, cols])
            xw = (xdt[:, gs] * jnp.exp(atot[:, gs] - acr[:, gs])).astype(BF16)
            s_scr[g] = sg * jnp.exp(atot[:, gs]) + _dot_tn(bg, xw)

    zg = z_ref[...].astype(F32)
    yg = y_scr[...] * _silu(zg)
    y_ref[...] = (yg * lax.rsqrt(jnp.mean(yg * yg, axis=-1, keepdims=True) + EPS)
                  * nw_ref[...]).astype(BF16)
    if emit_state:
        sfin_ref[...] = s_scr[...]
        tailfin_ref[...] = ext_scr[0:8, :]


def _ssd(proj, dt, s0, tail0, conv_w, conv_b, a_log, dsk_rep, norm_w, e_mat, *, nbatch, ts, emit_state):
    rows = proj.shape[0]
    nsteps = rows // (nbatch * ts)
    rb = lambda b, s: b * nsteps + s
    const = lambda b, s: (0, 0)
    in_specs = [
        pl.BlockSpec((ts, 1024), lambda b, s: (rb(b, s), COL_Z)),
        pl.BlockSpec((ts, 1024), lambda b, s: (rb(b, s), COL_XS)),
        pl.BlockSpec((ts, 256), lambda b, s: (rb(b, s), COL_B)),
        pl.BlockSpec((ts, 256), lambda b, s: (rb(b, s), COL_C)),
        pl.BlockSpec((ts, LANES), lambda b, s: (rb(b, s), 0)),
        pl.BlockSpec((SSD_GROUPS, SSD_STATE, 512), lambda b, s: (0, 0, 0)),
        pl.BlockSpec((8, SSD_XBC), const),
        pl.BlockSpec((SSD_CONV, SSD_XBC), const),
        pl.BlockSpec((1, SSD_XBC), const),
        pl.BlockSpec((1, LANES), const),
        pl.BlockSpec((1, SSD_WIDTH), const),
        pl.BlockSpec((1, SSD_WIDTH), const),
        pl.BlockSpec((LANES, SSD_WIDTH), const),
    ]
    out_specs = [pl.BlockSpec((ts, SSD_WIDTH), lambda b, s: (rb(b, s), 0))]
    out_shape = [jax.ShapeDtypeStruct((rows, SSD_WIDTH), BF16)]
    if emit_state:
        out_specs += [pl.BlockSpec((SSD_GROUPS, SSD_STATE, 512), lambda b, s: (0, 0, 0)),
                      pl.BlockSpec((8, SSD_XBC), const)]
        out_shape += [jax.ShapeDtypeStruct((SSD_GROUPS, SSD_STATE, 512), F32),
                      jax.ShapeDtypeStruct((8, SSD_XBC), F32)]
    return pl.pallas_call(
        functools.partial(_ssd_kernel, ts=ts, emit_state=emit_state),
        grid=(nbatch, nsteps),
        in_specs=in_specs,
        out_specs=out_specs,
        out_shape=out_shape,
        scratch_shapes=[
            pltpu.VMEM((SSD_GROUPS, SSD_STATE, 512), F32),
            pltpu.VMEM((ts + 8, SSD_XBC), F32),
            pltpu.VMEM((ts, SSD_XBC), F32),
            pltpu.VMEM((ts, SSD_WIDTH), F32),
        ],
        compiler_params=pltpu.CompilerParams(dimension_semantics=("arbitrary", "arbitrary")),
        name="ssd_meta" if emit_state else "ssd",
    )(proj, proj, proj, proj, dt, s0, tail0, conv_w, conv_b, a_log, dsk_rep, norm_w, e_mat)


def _sb_tile(qm, kb, vb, c, acc, tmat, mask):
    z = _dot_nt(qm, kb)
    sp = _softplus(z)
    if mask is not None:
        sp = jnp.where(mask, sp, 0.0)
    excl = _dot(sp.astype(BF16), tmat)
    w = jnp.exp(z - sp - excl - c)
    if mask is not None:
        w = jnp.where(mask, w, 0.0)
    acc = acc + _dot(w.astype(BF16), vb)
    c = c + jnp.sum(sp, axis=1, keepdims=True)
    return c, acc


def _attn_kernel(q_ref, k_ref, v_ref, km_ref, vm_ref, o_ref, *, nq):
    lane = lax.broadcasted_iota(jnp.int32, (TQ, PAIR), 1)
    low = lane < HEAD_DIM
    row = lax.broadcasted_iota(jnp.int32, (TQ, TQ), 0)
    col = lax.broadcasted_iota(jnp.int32, (TQ, TQ), 1)
    tmat = (row > col).astype(BF16)
    diag_mask = col < row
    tmat_meta = tmat[0:LANES, 0:LANES]
    meta_mask = lax.broadcasted_iota(jnp.int32, (TQ, LANES), 1) < N_META
    km = km_ref[...]
    vm = vm_ref[...]

    def qblock(qi, carry):
        r0 = pl.multiple_of(qi * TQ, TQ)
        q2 = q_ref[pl.ds(r0, TQ), :] * jnp.asarray(1.0 / math.sqrt(HEAD_DIM), BF16)
        zero = jnp.zeros_like(q2)
        qa = jnp.where(low, q2, zero)
        qb = jnp.where(low, zero, q2)
        kb = k_ref[pl.ds(r0, TQ), :]
        vb = v_ref[pl.ds(r0, TQ), :]
        c0 = jnp.zeros((TQ, 1), F32)
        a0 = jnp.zeros((TQ, PAIR), F32)
        ca, acca = _sb_tile(qa, kb, vb, c0, a0, tmat, diag_mask)
        cb, accb = _sb_tile(qb, kb, vb, c0, a0, tmat, diag_mask)

        def kblock(t, st):
            ca, acca, cb, accb = st
            k0 = pl.multiple_of((qi - 1 - t) * TQ, TQ)
            kb = k_ref[pl.ds(k0, TQ), :]
            vb = v_ref[pl.ds(k0, TQ), :]
            ca, acca = _sb_tile(qa, kb, vb, ca, acca, tmat, None)
            cb, accb = _sb_tile(qb, kb, vb, cb, accb, tmat, None)
            return ca, acca, cb, accb

        ca, acca, cb, accb = lax.fori_loop(0, qi, kblock, (ca, acca, cb, accb))
        ca, acca = _sb_tile(qa, km, vm, ca, acca, tmat_meta, meta_mask)
        cb, accb = _sb_tile(qb, km, vm, cb, accb, tmat_meta, meta_mask)
        o_ref[pl.ds(r0, TQ), :] = jnp.where(low, acca, accb).astype(BF16)
        return carry

    lax.fori_loop(0, nq, qblock, 0)


def _attn(proj, kmeta, vmeta, *, nbatch, seq):
    npair = SB_WIDTH // PAIR
    return pl.pallas_call(
        functools.partial(_attn_kernel, nq=seq // TQ),
        grid=(nbatch, npair),
        in_specs=[
            pl.BlockSpec((seq, PAIR), lambda b, p: (b, COL_Q * npair + p)),
            pl.BlockSpec((seq, PAIR), lambda b, p: (b, COL_K * npair + p)),
            pl.BlockSpec((seq, PAIR), lambda b, p: (b, COL_V * npair + p)),
            pl.BlockSpec((LANES, PAIR), lambda b, p: (0, p)),
            pl.BlockSpec((LANES, PAIR), lambda b, p: (0, p)),
        ],
        out_specs=pl.BlockSpec((seq, PAIR), lambda b, p: (b, p)),
        out_shape=jax.ShapeDtypeStruct((nbatch * seq, SB_WIDTH), BF16),
        compiler_params=pltpu.CompilerParams(dimension_semantics=("arbitrary", "arbitrary")),
        name="sb_attn",
    )(proj, proj, proj, kmeta, vmeta)


def _outproj_kernel(x_ref, osb_ref, gate_ref, yssd_ref, sbw_ref, wo_ref, fw_ref, o_ref):
    g = gate_ref[...].astype(F32)
    ys = osb_ref[...].astype(F32) * _silu(g)
    ysb = ys * lax.rsqrt(jnp.mean(ys * ys, axis=-1, keepdims=True) + EPS) * sbw_ref[...]
    h = (x_ref[...] + _dot(ysb.astype(BF16), wo_ref[0:SB_WIDTH, :])
         + _dot(yssd_ref[...], wo_ref[SB_WIDTH:SB_WIDTH + SSD_WIDTH, :]))
    o_ref[...] = h * lax.rsqrt(jnp.mean(h * h, axis=-1, keepdims=True) + EPS) * fw_ref[...]


def _outproj(x2d, osb, proj, yssd, sb_norm_w, w_out, final_w, *, tm=512):
    rows = x2d.shape[0]
    return pl.pallas_call(
        _outproj_kernel,
        grid=(rows // tm,),
        in_specs=[
            pl.BlockSpec((tm, D_MODEL), lambda i: (i, 0)),
            pl.BlockSpec((tm, SB_WIDTH), lambda i: (i, 0)),
            pl.BlockSpec((tm, SB_WIDTH), lambda i: (i, COL_GATE)),
            pl.BlockSpec((tm, SSD_WIDTH), lambda i: (i, 0)),
            pl.BlockSpec((1, SB_WIDTH), lambda i: (0, 0)),
            pl.BlockSpec((SB_WIDTH + SSD_WIDTH, D_MODEL), lambda i: (0, 0)),
            pl.BlockSpec((1, D_MODEL), lambda i: (0, 0)),
        ],
        out_specs=pl.BlockSpec((tm, D_MODEL), lambda i: (i, 0)),
        out_shape=jax.ShapeDtypeStruct((rows, D_MODEL), F32),
        compiler_params=pltpu.CompilerParams(dimension_semantics=("arbitrary",)),
        name="outproj",
    )(x2d, osb, proj, yssd, sb_norm_w, w_out, final_w)


def _layer(x2d, meta, norm_w, w_in, conv_w, conv_b, dt_bias, a_log, d_skip, sb_norm_w,
           ssd_norm_w, w_out, nbatch, seq):
    nh = SSD_HEADS
    w_main = w_in[:, :D_MAIN].astype(BF16)
    w_dt = jnp.pad(w_in[:, D_MAIN:], ((0, 0), (0, LANES - nh))).astype(BF16)
    dtb = jnp.pad(dt_bias, (0, LANES - nh)).reshape(1, LANES)
    alog = jnp.pad(a_log, (0, LANES - nh)).reshape(1, LANES)
    norm_w = norm_w.reshape(1, D_MODEL)
    dsk_rep = jnp.repeat(d_skip, HEAD_DIM).reshape(1, SSD_WIDTH)
    e_mat = (jnp.arange(LANES)[:, None] == (jnp.arange(SSD_WIDTH)[None, :] // HEAD_DIM)).astype(BF16)
    conv_b = conv_b.reshape(1, SSD_XBC)
    ssd_norm_w = ssd_norm_w.reshape(1, SSD_WIDTH)

    proj_m, dt_m = _inproj(meta, norm_w, w_main, w_dt, dtb, tm=N_META)
    pad = CHUNK - N_META
    proj_mp = jnp.pad(proj_m, ((pad, 0), (0, 0)))
    dt_mp = jnp.pad(dt_m, ((pad, 0), (0, 0)))
    zeros_s = jnp.zeros((SSD_GROUPS, SSD_STATE, 512), F32)
    zeros_t = jnp.zeros((8, SSD_XBC), F32)
    _, s_meta, tail_meta = _ssd(proj_mp, dt_mp, zeros_s, zeros_t, conv_w, conv_b, alog, dsk_rep,
                                ssd_norm_w, e_mat, nbatch=1, ts=CHUNK, emit_state=True)
    kmeta = jnp.pad(proj_m[:, COL_K * 1024:(COL_K + 1) * 1024], ((0, LANES - N_META), (0, 0)))
    vmeta = jnp.pad(proj_m[:, COL_V * 1024:(COL_V + 1) * 1024], ((0, LANES - N_META), (0, 0)))

    proj, dt = _inproj(x2d, norm_w, w_main, w_dt, dtb, tm=1024)
    (yssd,) = _ssd(proj, dt, s_meta, tail_meta, conv_w, conv_b, alog, dsk_rep, ssd_norm_w, e_mat,
                   nbatch=nbatch, ts=TQ, emit_state=False)
    osb = _attn(proj, kmeta, vmeta, nbatch=nbatch, seq=seq)
    return osb, proj, yssd


def kernel(x, meta_tokens, norm_w, w_in, conv_w, conv_b, dt_bias, a_log, d_skip, sb_norm_w,
           ssd_norm_w, w_out, final_norm_w):
    nbatch, seq, _ = x.shape
    assert norm_w.shape[0] == 1, "single-layer block"
    x2d = x.reshape(nbatch * seq, D_MODEL)
    osb, proj, yssd = _layer(x2d, meta_tokens, norm_w[0], w_in[0], conv_w[0], conv_b[0], dt_bias[0],
                             a_log[0], d_skip[0], sb_norm_w[0], ssd_norm_w[0], w_out[0], nbatch, seq)
    out = _outproj(x2d, osb, proj, yssd, sb_norm_w[0].reshape(1, SB_WIDTH), w_out[0].astype(BF16),
                   final_norm_w.reshape(1, D_MODEL))
    return out.reshape(nbatch, seq, D_MODEL)
```

```python
import functools
import math

import jax
import jax.numpy as jnp
from jax import lax
from jax.experimental import pallas as pl
from jax.experimental.pallas import tpu as pltpu

F32 = jnp.float32
BF16 = jnp.bfloat16

D_MODEL = 1024
N_META = 16
HEAD_DIM = 64
SB_WIDTH = 1024
SSD_WIDTH = 1024
SSD_HEADS = 16
SSD_GROUPS = 2
SSD_STATE = 128
SSD_CONV = 4
SSD_XBC = SSD_WIDTH + 2 * SSD_GROUPS * SSD_STATE
D_MAIN = 4 * SB_WIDTH + SSD_WIDTH + SSD_XBC
EPS = 1e-5

LANES = 128
PAIR = 2 * HEAD_DIM
CHUNK = 64
TQ = 256
NEG_BIG = -1e30
LOG2E = 1.4426950408889634
Q_SCALE = LOG2E / math.sqrt(HEAD_DIM)
C_SKIP = 160.0

COL_Q, COL_K, COL_V, COL_GATE, COL_Z, COL_XS = 0, 1, 2, 3, 4, 5
COL_B, COL_C = 24, 25


def _softplus(x):
    return jnp.maximum(x, 0.0) + jnp.log(1.0 + jnp.exp(-jnp.abs(x)))


def _silu(x):
    return x / (1.0 + jnp.exp(-x))


def _split3(x):
    hi = x.astype(BF16)
    r1 = x - hi.astype(F32)
    mid = r1.astype(BF16)
    lo = (r1 - mid.astype(F32)).astype(BF16)
    return hi, mid, lo


def _dot(a, b):
    return jnp.dot(a, b, preferred_element_type=F32)


def _dot_nt(a, b):
    return lax.dot_general(a, b, (((1,), (1,)), ((), ())), preferred_element_type=F32)


def _dot_tn(a, b):
    return lax.dot_general(a, b, (((0,), (0,)), ((), ())), preferred_element_type=F32)


def _inproj_kernel(x_ref, nw_ref, w_ref, wdt_ref, dtb_ref, proj_ref, dt_ref, u_ref, *, tn):
    @pl.when(pl.program_id(1) == 0)
    def _():
        x = x_ref[...]
        u = x * lax.rsqrt(jnp.mean(x * x, axis=-1, keepdims=True) + EPS) * nw_ref[...]
        ub = u.astype(BF16)
        u_ref[...] = ub
        dt_ref[...] = _softplus(_dot(ub, wdt_ref[...]) + dtb_ref[...])

    scale = jnp.where(pl.program_id(1) < SB_WIDTH // tn, Q_SCALE, 1.0).astype(F32)
    proj_ref[...] = (_dot(u_ref[...], w_ref[...]) * scale).astype(BF16)


def _inproj(x2d, norm_w, w_main, w_dt, dt_bias, *, tm, tn=512):
    rows = x2d.shape[0]
    return pl.pallas_call(
        functools.partial(_inproj_kernel, tn=tn),
        grid=(rows // tm, D_MAIN // tn),
        in_specs=[
            pl.BlockSpec((tm, D_MODEL), lambda i, j: (i, 0)),
            pl.BlockSpec((1, D_MODEL), lambda i, j: (0, 0)),
            pl.BlockSpec((D_MODEL, tn), lambda i, j: (0, j)),
            pl.BlockSpec((D_MODEL, LANES), lambda i, j: (0, 0)),
            pl.BlockSpec((1, LANES), lambda i, j: (0, 0)),
        ],
        out_specs=[
            pl.BlockSpec((tm, tn), lambda i, j: (i, j)),
            pl.BlockSpec((tm, LANES), lambda i, j: (i, 0)),
        ],
        out_shape=[
            jax.ShapeDtypeStruct((rows, D_MAIN), BF16),
            jax.ShapeDtypeStruct((rows, LANES), F32),
        ],
        scratch_shapes=[pltpu.VMEM((tm, D_MODEL), BF16)],
        compiler_params=pltpu.CompilerParams(dimension_semantics=("arbitrary", "arbitrary")),
        name="inproj",
    )(x2d, norm_w, w_main, w_dt, dt_bias)


def _ssd_kernel(z_ref, xs_ref, b_ref, c_ref, dt_ref, s0_ref, tail0_ref, cw_ref, cb_ref,
                alog_ref, dsk_ref, nw_ref, e_ref, *rest, ts, emit_state):
    if emit_state:
        y_ref, sfin_ref, tailfin_ref, s_scr, ext_scr, xbc_scr, y_scr = rest
    else:
        y_ref, s_scr, ext_scr, xbc_scr, y_scr = rest

    @pl.when(pl.program_id(1) == 0)
    def _():
        s_scr[...] = s0_ref[...]
        ext_scr[0:8, :] = tail0_ref[...]

    ext_scr[8:8 + ts, 0:SSD_WIDTH] = xs_ref[...].astype(F32)
    ext_scr[8:8 + ts, SSD_WIDTH:SSD_WIDTH + 256] = b_ref[...].astype(F32)
    ext_scr[8:8 + ts, SSD_WIDTH + 256:SSD_XBC] = c_ref[...].astype(F32)
    conv = cb_ref[...] + cw_ref[0:1, :] * ext_scr[5:5 + ts, :]
    for k in range(1, SSD_CONV):
        conv = conv + cw_ref[k:k + 1, :] * ext_scr[5 + k:5 + k + ts, :]
    xbc_scr[...] = _silu(conv)
    ext_scr[0:8, :] = ext_scr[ts:ts + 8, :]

    a_row = -jnp.exp(alog_ref[...])
    li = lax.broadcasted_iota(jnp.int32, (CHUNK, CHUNK), 0)
    si = lax.broadcasted_iota(jnp.int32, (CHUNK, CHUNK), 1)
    t_incl = (si <= li).astype(BF16)
    l2 = lax.broadcasted_iota(jnp.int32, (CHUNK, LANES), 0)
    lane2 = lax.broadcasted_iota(jnp.int32, (CHUNK, LANES), 1)
    mask2 = (lane2 % CHUNK) <= l2
    low_half = lane2 < HEAD_DIM
    e_mat = e_ref[...]

    for c in range(ts // CHUNK):
        r0 = c * CHUNK
        xs = xbc_scr[r0:r0 + CHUNK, 0:SSD_WIDTH]
        bm = xbc_scr[r0:r0 + CHUNK, SSD_WIDTH:SSD_WIDTH + 256]
        cm = xbc_scr[r0:r0 + CHUNK, SSD_WIDTH + 256:SSD_XBC]
        dt = dt_ref[r0:r0 + CHUNK, :]
        dta = dt * a_row
        h3 = _split3(dta)
        acum = _dot(t_incl, h3[0]) + _dot(t_incl, h3[1]) + _dot(t_incl, h3[2])
        st3 = _split3(jnp.concatenate([dt, acum], axis=0))
        rep = _dot(st3[0], e_mat) + _dot(st3[1], e_mat) + _dot(st3[2], e_mat)
        dtr = rep[0:CHUNK]
        acr = rep[CHUNK:2 * CHUNK]
        atot = acr[CHUNK - 1:CHUNK, :]
        wt = jnp.concatenate([acum, pltpu.roll(acum, LANES - 1, axis=1)], axis=0).T
        xdt = xs * dtr
        for g in range(SSD_GROUPS):
            gs = slice(g * 512, (g + 1) * 512)
            sg = s_scr[g]
            cg = cm[:, g * SSD_STATE:(g + 1) * SSD_STATE].astype(BF16)
            bg = bm[:, g * SSD_STATE:(g + 1) * SSD_STATE].astype(BF16)
            yoff = _dot(cg, sg.astype(BF16)) * jnp.exp(acr[:, gs])
            cb2 = _dot_nt(cg, jnp.concatenate([bg, bg], axis=0))
            for pp in range(4):
                p = 4 * g + pp
                cols = slice(p * PAIR, (p + 1) * PAIR)
                seg = acr[:, cols] - wt[2 * p:2 * p + 1, :]
                m2 = (cb2 * jnp.exp(jnp.where(mask2, seg, NEG_BIG))).astype(BF16)
                x2 = xdt[:, cols].astype(BF16)
                zero = jnp.zeros_like(x2)
                xblk = jnp.concatenate([jnp.where(low_half, x2, zero),
                                        jnp.where(low_half, zero, x2)], axis=0)
                y_scr[r0:r0 + CHUNK, cols] = (_dot(m2, xblk) + yoff[:, pp * PAIR:(pp + 1) * PAIR]
                                              + xs[:, cols] * dsk_ref[:, cols])
            xw = (xdt[:, gs] * jnp.exp(atot[:, gs] - acr[:, gs])).astype(BF16)
            s_scr[g] = sg * jnp.exp(atot[:, gs]) + _dot_tn(bg, xw)

    zg = z_ref[...].astype(F32)
    yg = y_scr[...] * _silu(zg)
    y_ref[...] = (yg * lax.rsqrt(jnp.mean(yg * yg, axis=-1, keepdims=True) + EPS)
                  * nw_ref[...]).astype(BF16)
    if emit_state:
        sfin_ref[...] = s_scr[...]
        tailfin_ref[...] = ext_scr[0:8, :]


def _ssd(proj, dt, s0, tail0, conv_w, conv_b, a_log, dsk_rep, norm_w, e_mat, *, nbatch, ts, emit_state):
    rows = proj.shape[0]
    nsteps = rows // (nbatch * ts)
    rb = lambda b, s: b * nsteps + s
    const = lambda b, s: (0, 0)
    in_specs = [
        pl.BlockSpec((ts, 1024), lambda b, s: (rb(b, s), COL_Z)),
        pl.BlockSpec((ts, 1024), lambda b, s: (rb(b, s), COL_XS)),
        pl.BlockSpec((ts, 256), lambda b, s: (rb(b, s), COL_B)),
        pl.BlockSpec((ts, 256), lambda b, s: (rb(b, s), COL_C)),
        pl.BlockSpec((ts, LANES), lambda b, s: (rb(b, s), 0)),
        pl.BlockSpec((SSD_GROUPS, SSD_STATE, 512), lambda b, s: (0, 0, 0)),
        pl.BlockSpec((8, SSD_XBC), const),
        pl.BlockSpec((SSD_CONV, SSD_XBC), const),
        pl.BlockSpec((1, SSD_XBC), const),
        pl.BlockSpec((1, LANES), const),
        pl.BlockSpec((1, SSD_WIDTH), const),
        pl.BlockSpec((1, SSD_WIDTH), const),
        pl.BlockSpec((LANES, SSD_WIDTH), const),
    ]
    out_specs = [pl.BlockSpec((ts, SSD_WIDTH), lambda b, s: (rb(b, s), 0))]
    out_shape = [jax.ShapeDtypeStruct((rows, SSD_WIDTH), BF16)]
    if emit_state:
        out_specs += [pl.BlockSpec((SSD_GROUPS, SSD_STATE, 512), lambda b, s: (0, 0, 0)),
                      pl.BlockSpec((8, SSD_XBC), const)]
        out_shape += [jax.ShapeDtypeStruct((SSD_GROUPS, SSD_STATE, 512), F32),
                      jax.ShapeDtypeStruct((8, SSD_XBC), F32)]
    return pl.pallas_call(
        functools.partial(_ssd_kernel, ts=ts, emit_state=emit_state),
        grid=(nbatch, nsteps),
        in_specs=in_specs,
        out_specs=out_specs,
        out_shape=out_shape,
        scratch_shapes=[
            pltpu.VMEM((SSD_GROUPS, SSD_STATE, 512), F32),
            pltpu.VMEM((ts + 8, SSD_XBC), F32),
            pltpu.VMEM((ts, SSD_XBC), F32),
            pltpu.VMEM((ts, SSD_WIDTH), F32),
        ],
        compiler_params=pltpu.CompilerParams(dimension_semantics=("arbitrary", "arbitrary")),
        name="ssd_meta" if emit_state else "ssd",
    )(proj, proj, proj, proj, dt, s0, tail0, conv_w, conv_b, a_log, dsk_rep, norm_w, e_mat)


def _sb_tile(qm, kb, vb, c, acc, tmat, mask):
    y = _dot_nt(qm, kb)
    sp = jnp.maximum(y, 0.0) + jnp.log(1.0 + jnp.exp2(-jnp.abs(y))) * LOG2E
    if mask is not None:
        sp = jnp.where(mask, sp, 0.0)
    excl = _dot(sp.astype(BF16), tmat)
    e = y - sp - excl
    if c is not None:
        e = e - c
    w = jnp.exp2(e)
    if mask is not None:
        w = jnp.where(mask, w, 0.0)
    pv = _dot(w.astype(BF16), vb)
    rs = jnp.sum(sp, axis=1, keepdims=True)
    if c is None:
        return rs, pv
    return c + rs, acc + pv


def _attn_kernel(q_ref, k_ref, v_ref, km_ref, vm_ref, o_ref, c_scr, acc_scr, *, nq, npair):
    nchain = 2 * npair
    lane = lax.broadcasted_iota(jnp.int32, (TQ, PAIR), 1)
    low = lane < HEAD_DIM
    row = lax.broadcasted_iota(jnp.int32, (TQ, TQ), 0)
    col = lax.broadcasted_iota(jnp.int32, (TQ, TQ), 1)
    tmat = (row > col).astype(BF16)
    diag_mask = col < row
    tmat_meta = tmat[0:LANES, 0:LANES]
    meta_mask = lax.broadcasted_iota(jnp.int32, (TQ, LANES), 1) < N_META

    def cols(p):
        return slice(p * PAIR, (p + 1) * PAIR)

    def tiles(qs, kget, vget, tm, mask, first=False):
        for i in range(nchain):
            c, a = _sb_tile(qs[i], kget(i // 2), vget(i // 2),
                            None if first else c_scr[i], None if first else acc_scr[i], tm, mask)
            c_scr[i] = c
            acc_scr[i] = a

    def unfinished():
        m = c_scr[0]
        for i in range(1, nchain):
            m = jnp.minimum(m, c_scr[i])
        return (jnp.min(m) < C_SKIP).astype(jnp.int32)

    def qblock(qi, carry):
        r0 = pl.multiple_of(qi * TQ, TQ)
        qs = []
        for p in range(npair):
            q2 = q_ref[pl.ds(r0, TQ), cols(p)]
            zero = jnp.zeros_like(q2)
            qs += [jnp.where(low, q2, zero), jnp.where(low, zero, q2)]
        tiles(qs, lambda p: k_ref[pl.ds(r0, TQ), cols(p)], lambda p: v_ref[pl.ds(r0, TQ), cols(p)],
              tmat, diag_mask, first=True)

        def cond(st):
            return jnp.logical_and(st[0] < qi, st[1] > 0)

        def body(st):
            k0 = pl.multiple_of((qi - 1 - st[0]) * TQ, TQ)
            tiles(qs, lambda p: k_ref[pl.ds(k0, TQ), cols(p)], lambda p: v_ref[pl.ds(k0, TQ), cols(p)],
                  tmat, None)
            return st[0] + 1, unfinished()

        _, more = lax.while_loop(cond, body, (jnp.int32(0), unfinished()))

        @pl.when(more > 0)
        def _():
            tiles(qs, lambda p: km_ref[:, cols(p)], lambda p: vm_ref[:, cols(p)], tmat_meta, meta_mask)

        for p in range(npair):
            o_ref[pl.ds(r0, TQ), cols(p)] = jnp.where(low, acc_scr[2 * p], acc_scr[2 * p + 1]).astype(BF16)
        return carry

    lax.fori_loop(0, nq, qblock, 0)


def _attn(proj, kmeta, vmeta, *, nbatch, seq, npair=2):
    width = npair * PAIR
    nblk = SB_WIDTH // width
    return pl.pallas_call(
        functools.partial(_attn_kernel, nq=seq // TQ, npair=npair),
        grid=(nbatch, nblk),
        in_specs=[
            pl.BlockSpec((seq, width), lambda b, p: (b, COL_Q * nblk + p)),
            pl.BlockSpec((seq, width), lambda b, p: (b, COL_K * nblk + p)),
            pl.BlockSpec((seq, width), lambda b, p: (b, COL_V * nblk + p)),
            pl.BlockSpec((LANES, width), lambda b, p: (0, p)),
            pl.BlockSpec((LANES, width), lambda b, p: (0, p)),
        ],
        out_specs=pl.BlockSpec((seq, width), lambda b, p: (b, p)),
        out_shape=jax.ShapeDtypeStruct((nbatch * seq, SB_WIDTH), BF16),
        scratch_shapes=[pltpu.VMEM((2 * npair, TQ, 1), F32), pltpu.VMEM((2 * npair, TQ, PAIR), F32)],
        compiler_params=pltpu.CompilerParams(dimension_semantics=("arbitrary", "arbitrary")),
        name="sb_attn",
    )(proj, proj, proj, kmeta, vmeta)


def _outproj_kernel(x_ref, osb_ref, gate_ref, yssd_ref, sbw_ref, wo_ref, fw_ref, o_ref):
    g = gate_ref[...].astype(F32)
    ys = osb_ref[...].astype(F32) * _silu(g)
    ysb = ys * lax.rsqrt(jnp.mean(ys * ys, axis=-1, keepdims=True) + EPS) * sbw_ref[...]
    h = (x_ref[...] + _dot(ysb.astype(BF16), wo_ref[0:SB_WIDTH, :])
         + _dot(yssd_ref[...], wo_ref[SB_WIDTH:SB_WIDTH + SSD_WIDTH, :]))
    o_ref[...] = h * lax.rsqrt(jnp.mean(h * h, axis=-1, keepdims=True) + EPS) * fw_ref[...]


def _outproj(x2d, osb, proj, yssd, sb_norm_w, w_out, final_w, *, tm=512):
    rows = x2d.shape[0]
    return pl.pallas_call(
        _outproj_kernel,
        grid=(rows // tm,),
        in_specs=[
            pl.BlockSpec((tm, D_MODEL), lambda i: (i, 0)),
            pl.BlockSpec((tm, SB_WIDTH), lambda i: (i, 0)),
            pl.BlockSpec((tm, SB_WIDTH), lambda i: (i, COL_GATE)),
            pl.BlockSpec((tm, SSD_WIDTH), lambda i: (i, 0)),
            pl.BlockSpec((1, SB_WIDTH), lambda i: (0, 0)),
            pl.BlockSpec((SB_WIDTH + SSD_WIDTH, D_MODEL), lambda i: (0, 0)),
            pl.BlockSpec((1, D_MODEL), lambda i: (0, 0)),
        ],
        out_specs=pl.BlockSpec((tm, D_MODEL), lambda i: (i, 0)),
        out_shape=jax.ShapeDtypeStruct((rows, D_MODEL), F32),
        compiler_params=pltpu.CompilerParams(dimension_semantics=("arbitrary",)),
        name="outproj",
    )(x2d, osb, proj, yssd, sb_norm_w, w_out, final_w)


def _layer(x2d, meta, norm_w, w_in, conv_w, conv_b, dt_bias, a_log, d_skip, sb_norm_w,
           ssd_norm_w, w_out, nbatch, seq):
    nh = SSD_HEADS
    w_main = w_in[:, :D_MAIN].astype(BF16)
    w_dt = jnp.pad(w_in[:, D_MAIN:], ((0, 0), (0, LANES - nh))).astype(BF16)
    dtb = jnp.pad(dt_bias, (0, LANES - nh)).reshape(1, LANES)
    alog = jnp.pad(a_log, (0, LANES - nh)).reshape(1, LANES)
    norm_w = norm_w.reshape(1, D_MODEL)
    dsk_rep = jnp.repeat(d_skip, HEAD_DIM).reshape(1, SSD_WIDTH)
    e_mat = (jnp.arange(LANES)[:, None] == (jnp.arange(SSD_WIDTH)[None, :] // HEAD_DIM)).astype(BF16)
    conv_b = conv_b.reshape(1, SSD_XBC)
    ssd_norm_w = ssd_norm_w.reshape(1, SSD_WIDTH)

    proj_m, dt_m = _inproj(meta, norm_w, w_main, w_dt, dtb, tm=N_META)
    pad = CHUNK - N_META
    proj_mp = jnp.pad(proj_m, ((pad, 0), (0, 0)))
    dt_mp = jnp.pad(dt_m, ((pad, 0), (0, 0)))
    zeros_s = jnp.zeros((SSD_GROUPS, SSD_STATE, 512), F32)
    zeros_t = jnp.zeros((8, SSD_XBC), F32)
    _, s_meta, tail_meta = _ssd(proj_mp, dt_mp, zeros_s, zeros_t, conv_w, conv_b, alog, dsk_rep,
                                ssd_norm_w, e_mat, nbatch=1, ts=CHUNK, emit_state=True)
    kmeta = jnp.pad(proj_m[:, COL_K * 1024:(COL_K + 1) * 1024], ((0, LANES - N_META), (0, 0)))
    vmeta = jnp.pad(proj_m[:, COL_V * 1024:(COL_V + 1) * 1024], ((0, LANES - N_META), (0, 0)))

    proj, dt = _inproj(x2d, norm_w, w_main, w_dt, dtb, tm=1024)
    (yssd,) = _ssd(proj, dt, s_meta, tail_meta, conv_w, conv_b, alog, dsk_rep, ssd_norm_w, e_mat,
                   nbatch=nbatch, ts=TQ, emit_state=False)
    osb = _attn(proj, kmeta, vmeta, nbatch=nbatch, seq=seq)
    return osb, proj, yssd


def kernel(x, meta_tokens, norm_w, w_in, conv_w, conv_b, dt_bias, a_log, d_skip, sb_norm_w,
           ssd_norm_w, w_out, final_norm_w):
    nbatch, seq, _ = x.shape
    assert norm_w.shape[0] == 1, "single-layer block"
    x2d = x.reshape(nbatch * seq, D_MODEL)
    osb, proj, yssd = _layer(x2d, meta_tokens, norm_w[0], w_in[0], conv_w[0], conv_b[0], dt_bias[0],
                             a_log[0], d_skip[0], sb_norm_w[0], ssd_norm_w[0], w_out[0], nbatch, seq)
    out = _outproj(x2d, osb, proj, yssd, sb_norm_w[0].reshape(1, SB_WIDTH), w_out[0].astype(BF16),
                   final_norm_w.reshape(1, D_MODEL))
    return out.reshape(nbatch, seq, D_MODEL)
```

```python
import functools
import math

import jax
import jax.numpy as jnp
from jax import lax
from jax.experimental import pallas as pl
from jax.experimental.pallas import tpu as pltpu

F32 = jnp.float32
BF16 = jnp.bfloat16

D_MODEL = 1024
N_META = 16
HEAD_DIM = 64
SB_WIDTH = 1024
SSD_WIDTH = 1024
SSD_HEADS = 16
SSD_GROUPS = 2
SSD_STATE = 128
SSD_CONV = 4
SSD_XBC = SSD_WIDTH + 2 * SSD_GROUPS * SSD_STATE
D_MAIN = 4 * SB_WIDTH + SSD_WIDTH + SSD_XBC
EPS = 1e-5

LANES = 128
PAIR = 2 * HEAD_DIM
CHUNK = 64
TQ = 256
NEG_BIG = -1e30
LOG2E = 1.4426950408889634
Q_SCALE = LOG2E / math.sqrt(HEAD_DIM)
C_SKIP = 160.0

COL_Q, COL_K, COL_V, COL_GATE, COL_Z, COL_XS = 0, 1, 2, 3, 4, 5
COL_B, COL_C = 24, 25


def _softplus(x):
    return jnp.maximum(x, 0.0) + jnp.log(1.0 + jnp.exp(-jnp.abs(x)))


def _silu(x):
    return x / (1.0 + jnp.exp(-x))


def _split3(x):
    hi = x.astype(BF16)
    r1 = x - hi.astype(F32)
    mid = r1.astype(BF16)
    lo = (r1 - mid.astype(F32)).astype(BF16)
    return hi, mid, lo


def _dot(a, b):
    return jnp.dot(a, b, preferred_element_type=F32)


def _dot_nt(a, b):
    return lax.dot_general(a, b, (((1,), (1,)), ((), ())), preferred_element_type=F32)


def _dot_tn(a, b):
    return lax.dot_general(a, b, (((0,), (0,)), ((), ())), preferred_element_type=F32)


def _inproj_kernel(x_ref, nw_ref, w_ref, wdt_ref, dtb_ref, proj_ref, dt_ref, *, tm, sub, tn):
    for r0 in range(0, tm, sub):
        x = x_ref[r0:r0 + sub, :]
        u = x * lax.rsqrt(jnp.mean(x * x, axis=-1, keepdims=True) + EPS) * nw_ref[...]
        ub = u.astype(BF16)
        dt_ref[r0:r0 + sub, :] = _softplus(_dot(ub, wdt_ref[...]) + dtb_ref[...])
        for c0 in range(0, D_MAIN, tn):
            acc = _dot(ub, w_ref[:, c0:c0 + tn])
            if c0 < SB_WIDTH:
                acc = acc * Q_SCALE
            proj_ref[r0:r0 + sub, c0:c0 + tn] = acc.astype(BF16)


def _inproj(x2d, norm_w, w_all, w_dt, dt_bias, *, tm, tn=512):
    rows = x2d.shape[0]
    sub = min(tm, 256)
    assert SB_WIDTH % tn == 0 and D_MAIN % tn == 0
    return pl.pallas_call(
        functools.partial(_inproj_kernel, tm=tm, sub=sub, tn=tn),
        grid=(rows // tm,),
        in_specs=[
            pl.BlockSpec((tm, D_MODEL), lambda i: (i, 0)),
            pl.BlockSpec((1, D_MODEL), lambda i: (0, 0)),
            pl.BlockSpec((D_MODEL, D_MAIN), lambda i: (0, 0), pipeline_mode=pl.Buffered(1)),
            pl.BlockSpec((D_MODEL, LANES), lambda i: (0, 0)),
            pl.BlockSpec((1, LANES), lambda i: (0, 0)),
        ],
        out_specs=[
            pl.BlockSpec((tm, D_MAIN), lambda i: (i, 0)),
            pl.BlockSpec((tm, LANES), lambda i: (i, 0)),
        ],
        out_shape=[
            jax.ShapeDtypeStruct((rows, D_MAIN), BF16),
            jax.ShapeDtypeStruct((rows, LANES), F32),
        ],
        compiler_params=pltpu.CompilerParams(dimension_semantics=("arbitrary",)),
        name="inproj",
    )(x2d, norm_w, w_all, w_dt, dt_bias)


def _ssd_kernel(z_ref, xs_ref, b_ref, c_ref, dt_ref, s0_ref, tail0_ref, cw_ref, cb_ref,
                alog_ref, dsk_ref, nw_ref, e_ref, *rest, ts, emit_state):
    if emit_state:
        y_ref, sfin_ref, tailfin_ref, s_scr, ext_scr, xbc_scr, y_scr = rest
    else:
        y_ref, s_scr, ext_scr, xbc_scr, y_scr = rest

    @pl.when(pl.program_id(1) == 0)
    def _():
        s_scr[...] = s0_ref[...]
        ext_scr[0:8, :] = tail0_ref[...]

    ext_scr[8:8 + ts, 0:SSD_WIDTH] = xs_ref[...].astype(F32)
    ext_scr[8:8 + ts, SSD_WIDTH:SSD_WIDTH + 256] = b_ref[...].astype(F32)
    ext_scr[8:8 + ts, SSD_WIDTH + 256:SSD_XBC] = c_ref[...].astype(F32)
    conv = cb_ref[...] + cw_ref[0:1, :] * ext_scr[5:5 + ts, :]
    for k in range(1, SSD_CONV):
        conv = conv + cw_ref[k:k + 1, :] * ext_scr[5 + k:5 + k + ts, :]
    xbc_scr[...] = _silu(conv)
    ext_scr[0:8, :] = ext_scr[ts:ts + 8, :]

    a_row = -jnp.exp(alog_ref[...])
    li = lax.broadcasted_iota(jnp.int32, (CHUNK, CHUNK), 0)
    si = lax.broadcasted_iota(jnp.int32, (CHUNK, CHUNK), 1)
    t_incl = (si <= li).astype(BF16)
    l2 = lax.broadcasted_iota(jnp.int32, (CHUNK, LANES), 0)
    lane2 = lax.broadcasted_iota(jnp.int32, (CHUNK, LANES), 1)
    mask2 = (lane2 % CHUNK) <= l2
    low_half = lane2 < HEAD_DIM
    e_mat = e_ref[...]

    for c in range(ts // CHUNK):
        r0 = c * CHUNK
        xs = xbc_scr[r0:r0 + CHUNK, 0:SSD_WIDTH]
        bm = xbc_scr[r0:r0 + CHUNK, SSD_WIDTH:SSD_WIDTH + 256]
        cm = xbc_scr[r0:r0 + CHUNK, SSD_WIDTH + 256:SSD_XBC]
        dt = dt_ref[r0:r0 + CHUNK, :]
        dta = dt * a_row
        h3 = _split3(dta)
        acum = _dot(t_incl, h3[0]) + _dot(t_incl, h3[1]) + _dot(t_incl, h3[2])
        st3 = _split3(jnp.concatenate([dt, acum], axis=0))
        rep = _dot(st3[0], e_mat) + _dot(st3[1], e_mat) + _dot(st3[2], e_mat)
        dtr = rep[0:CHUNK]
        acr = rep[CHUNK:2 * CHUNK]
        atot = acr[CHUNK - 1:CHUNK, :]
        wt = jnp.concatenate([acum, pltpu.roll(acum, LANES - 1, axis=1)], axis=0).T
        xdt = xs * dtr
        for g in range(SSD_GROUPS):
            gs = slice(g * 512, (g + 1) * 512)
            sg = s_scr[g]
            cg = cm[:, g * SSD_STATE:(g + 1) * SSD_STATE].astype(BF16)
            bg = bm[:, g * SSD_STATE:(g + 1) * SSD_STATE].astype(BF16)
            yoff = _dot(cg, sg.astype(BF16)) * jnp.exp(acr[:, gs])
            cb2 = _dot_nt(cg, jnp.concatenate([bg, bg], axis=0))
            for pp in range(4):
                p = 4 * g + pp
                cols = slice(p * PAIR, (p + 1) * PAIR)
                seg = acr[:, cols] - wt[2 * p:2 * p + 1, :]
                m2 = (cb2 * jnp.exp(jnp.where(mask2, seg, NEG_BIG))).astype(BF16)
                x2 = xdt[:, cols].astype(BF16)
                zero = jnp.zeros_like(x2)
                xblk = jnp.concatenate([jnp.where(low_half, x2, zero),
                                        jnp.where(low_half, zero, x2)], axis=0)
                y_scr[r0:r0 + CHUNK, cols] = (_dot(m2, xblk) + yoff[:, pp * PAIR:(pp + 1) * PAIR]
                                              + xs[:, cols] * dsk_ref[:, cols])
            xw = (xdt[:, gs] * jnp.exp(atot[:, gs] - acr[:, gs])).astype(BF16)
            s_scr[g] = sg * jnp.exp(atot[:, gs]) + _dot_tn(bg, xw)

    zg = z_ref[...].astype(F32)
    yg = y_scr[...] * _silu(zg)
    y_ref[...] = (yg * lax.rsqrt(jnp.mean(yg * yg, axis=-1, keepdims=True) + EPS)
                  * nw_ref[...]).astype(BF16)
    if emit_state:
        sfin_ref[...] = s_scr[...]
        tailfin_ref[...] = ext_scr[0:8, :]


def _ssd(proj, dt, s0, tail0, conv_w, conv_b, a_log, dsk_rep, norm_w, e_mat, *, nbatch, ts, emit_state):
    rows = proj.shape[0]
    nsteps = rows // (nbatch * ts)
    rb = lambda b, s: b * nsteps + s
    const = lambda b, s: (0, 0)
    in_specs = [
        pl.BlockSpec((ts, 1024), lambda b, s: (rb(b, s), COL_Z)),
        pl.BlockSpec((ts, 1024), lambda b, s: (rb(b, s), COL_XS)),
        pl.BlockSpec((ts, 256), lambda b, s: (rb(b, s), COL_B)),
        pl.BlockSpec((ts, 256), lambda b, s: (rb(b, s), COL_C)),
        pl.BlockSpec((ts, LANES), lambda b, s: (rb(b, s), 0)),
        pl.BlockSpec((SSD_GROUPS, SSD_STATE, 512), lambda b, s: (0, 0, 0)),
        pl.BlockSpec((8, SSD_XBC), const),
        pl.BlockSpec((SSD_CONV, SSD_XBC), const),
        pl.BlockSpec((1, SSD_XBC), const),
        pl.BlockSpec((1, LANES), const),
        pl.BlockSpec((1, SSD_WIDTH), const),
        pl.BlockSpec((1, SSD_WIDTH), const),
        pl.BlockSpec((LANES, SSD_WIDTH), const),
    ]
    out_specs = [pl.BlockSpec((ts, SSD_WIDTH), lambda b, s: (rb(b, s), 0))]
    out_shape = [jax.ShapeDtypeStruct((rows, SSD_WIDTH), BF16)]
    if emit_state:
        out_specs += [pl.BlockSpec((SSD_GROUPS, SSD_STATE, 512), lambda b, s: (0, 0, 0)),
                      pl.BlockSpec((8, SSD_XBC), const)]
        out_shape += [jax.ShapeDtypeStruct((SSD_GROUPS, SSD_STATE, 512), F32),
                      jax.ShapeDtypeStruct((8, SSD_XBC), F32)]
    return pl.pallas_call(
        functools.partial(_ssd_kernel, ts=ts, emit_state=emit_state),
        grid=(nbatch, nsteps),
        in_specs=in_specs,
        out_specs=out_specs,
        out_shape=out_shape,
        scratch_shapes=[
            pltpu.VMEM((SSD_GROUPS, SSD_STATE, 512), F32),
            pltpu.VMEM((ts + 8, SSD_XBC), F32),
            pltpu.VMEM((ts, SSD_XBC), F32),
            pltpu.VMEM((ts, SSD_WIDTH), F32),
        ],
        compiler_params=pltpu.CompilerParams(dimension_semantics=("arbitrary", "arbitrary")),
        name="ssd_meta" if emit_state else "ssd",
    )(proj, proj, proj, proj, dt, s0, tail0, conv_w, conv_b, a_log, dsk_rep, norm_w, e_mat)


def _sb_tiles(items, tmat):
    n = len(items)
    ys = [_dot_nt(it[0], it[1]) for it in items]
    sps = []
    for y, it in zip(ys, items):
        sp = jnp.maximum(y, 0.0) + jnp.log(1.0 + jnp.exp2(-jnp.abs(y))) * LOG2E
        if it[3] is not None:
            sp = jnp.where(it[3], sp, 0.0)
        sps.append(sp)
    excls = [_dot(sp.astype(BF16), tmat[0:sp.shape[1], 0:sp.shape[1]]) for sp in sps]
    rss = [excls[i][:, 0:1] + sps[i][:, 0:1] for i in range(n)]
    c_in, c_out = [], []
    for i, it in enumerate(items):
        c = c_out[it[6]] if it[6] is not None else it[4]
        c_in.append(c)
        c_out.append(rss[i] if c is None else c + rss[i])
    ws = []
    for i, it in enumerate(items):
        e = ys[i] - sps[i] - excls[i]
        if c_in[i] is not None:
            e = e - c_in[i]
        w = jnp.exp2(e)
        if it[3] is not None:
            w = jnp.where(it[3], w, 0.0)
        ws.append(w.astype(BF16))
    pvs = [_dot(ws[i], items[i][2]) for i in range(n)]
    a_out = []
    for i, it in enumerate(items):
        a = a_out[it[6]] if it[6] is not None else it[5]
        a_out.append(pvs[i] if a is None else a + pvs[i])
    return c_out, a_out


def _attn_kernel(q_ref, k_ref, v_ref, km_ref, vm_ref, o_ref, c_scr, acc_scr, *, nq, npair):
    nchain = 2 * npair
    lane = lax.broadcasted_iota(jnp.int32, (TQ, PAIR), 1)
    low = lane < HEAD_DIM
    row = lax.broadcasted_iota(jnp.int32, (TQ, TQ), 0)
    col = lax.broadcasted_iota(jnp.int32, (TQ, TQ), 1)
    tmat = (row > col).astype(BF16)
    diag_mask = col < row
    meta_mask = lax.broadcasted_iota(jnp.int32, (TQ, LANES), 1) < N_META

    def cols(p):
        return slice(p * PAIR, (p + 1) * PAIR)

    def load_q(r0):
        qs = []
        for p in range(npair):
            q2 = q_ref[pl.ds(r0, TQ), cols(p)]
            zero = jnp.zeros_like(q2)
            qs += [jnp.where(low, q2, zero), jnp.where(low, zero, q2)]
        return qs

    def kv_tile(qs, k0, mask, state):
        items = []
        for i in range(nchain):
            kb = k_ref[pl.ds(k0, TQ), cols(i // 2)]
            vb = v_ref[pl.ds(k0, TQ), cols(i // 2)]
            if state == "scratch":
                items.append((qs[i], kb, vb, mask, c_scr[i], acc_scr[i], None))
            elif state == "fresh":
                items.append((qs[i], kb, vb, mask, None, None, None))
            else:
                items.append((qs[i], kb, vb, mask, None, None, state + i))
        return items

    def commit(cs, accs):
        for i in range(nchain):
            c_scr[i] = cs[i]
            acc_scr[i] = accs[i]

    def unfinished():
        m = c_scr[0]
        for i in range(1, nchain):
            m = jnp.minimum(m, c_scr[i])
        return (jnp.min(m) < C_SKIP).astype(jnp.int32)

    def meta_tile(qs):
        items = [(qs[i], km_ref[:, cols(i // 2)], vm_ref[:, cols(i // 2)], meta_mask,
                  c_scr[i], acc_scr[i], None) for i in range(nchain)]
        commit(*_sb_tiles(items, tmat))

    def write_out(r0):
        for p in range(npair):
            o_ref[pl.ds(r0, TQ), cols(p)] = jnp.where(low, acc_scr[2 * p], acc_scr[2 * p + 1]).astype(BF16)

    qs0 = load_q(0)
    commit(*_sb_tiles(kv_tile(qs0, 0, diag_mask, "fresh"), tmat))
    meta_tile(qs0)
    write_out(0)

    def qblock(qi, carry):
        r0 = pl.multiple_of(qi * TQ, TQ)
        qs = load_q(r0)
        items = (kv_tile(qs, r0, diag_mask, "fresh")
                 + kv_tile(qs, pl.multiple_of(r0 - TQ, TQ), None, 0))
        cs, accs = _sb_tiles(items, tmat)
        commit(cs[nchain:], accs[nchain:])

        def cond(st):
            return jnp.logical_and(st[0] < qi, st[1] > 0)

        def body(st):
            k0 = pl.multiple_of((qi - 1 - st[0]) * TQ, TQ)
            commit(*_sb_tiles(kv_tile(qs, k0, None, "scratch"), tmat))
            return st[0] + 1, unfinished()

        _, more = lax.while_loop(cond, body, (jnp.int32(1), unfinished()))

        @pl.when(more > 0)
        def _():
            meta_tile(qs)

        write_out(r0)
        return carry

    lax.fori_loop(1, nq, qblock, 0)


def _attn(proj, kmeta, vmeta, *, nbatch, seq, npair=2):
    width = npair * PAIR
    nblk = SB_WIDTH // width
    return pl.pallas_call(
        functools.partial(_attn_kernel, nq=seq // TQ, npair=npair),
        grid=(nbatch, nblk),
        in_specs=[
            pl.BlockSpec((seq, width), lambda b, p: (b, COL_Q * nblk + p)),
            pl.BlockSpec((seq, width), lambda b, p: (b, COL_K * nblk + p)),
            pl.BlockSpec((seq, width), lambda b, p: (b, COL_V * nblk + p)),
            pl.BlockSpec((LANES, width), lambda b, p: (0, p)),
            pl.BlockSpec((LANES, width), lambda b, p: (0, p)),
        ],
        out_specs=pl.BlockSpec((seq, width), lambda b, p: (b, p)),
        out_shape=jax.ShapeDtypeStruct((nbatch * seq, SB_WIDTH), BF16),
        scratch_shapes=[pltpu.VMEM((2 * npair, TQ, 1), F32), pltpu.VMEM((2 * npair, TQ, PAIR), F32)],
        compiler_params=pltpu.CompilerParams(dimension_semantics=("arbitrary", "arbitrary")),
        name="sb_attn",
    )(proj, proj, proj, kmeta, vmeta)


def _outproj_kernel(x_ref, osb_ref, gate_ref, yssd_ref, sbw_ref, wo_ref, fw_ref, o_ref, *, tm, sub):
    for r0 in range(0, tm, sub):
        rows = slice(r0, r0 + sub)
        g = gate_ref[rows, :].astype(F32)
        ys = osb_ref[rows, :].astype(F32) * _silu(g)
        ysb = ys * lax.rsqrt(jnp.mean(ys * ys, axis=-1, keepdims=True) + EPS) * sbw_ref[...]
        h = (x_ref[rows, :] + _dot(ysb.astype(BF16), wo_ref[0:SB_WIDTH, :])
             + _dot(yssd_ref[rows, :], wo_ref[SB_WIDTH:SB_WIDTH + SSD_WIDTH, :]))
        o_ref[rows, :] = h * lax.rsqrt(jnp.mean(h * h, axis=-1, keepdims=True) + EPS) * fw_ref[...]


def _outproj(x2d, osb, proj, yssd, sb_norm_w, w_out, final_w, *, tm=512):
    rows = x2d.shape[0]
    return pl.pallas_call(
        functools.partial(_outproj_kernel, tm=tm, sub=256),
        grid=(rows // tm,),
        in_specs=[
            pl.BlockSpec((tm, D_MODEL), lambda i: (i, 0)),
            pl.BlockSpec((tm, SB_WIDTH), lambda i: (i, 0)),
            pl.BlockSpec((tm, SB_WIDTH), lambda i: (i, COL_GATE)),
            pl.BlockSpec((tm, SSD_WIDTH), lambda i: (i, 0)),
            pl.BlockSpec((1, SB_WIDTH), lambda i: (0, 0)),
            pl.BlockSpec((SB_WIDTH + SSD_WIDTH, D_MODEL), lambda i: (0, 0)),
            pl.BlockSpec((1, D_MODEL), lambda i: (0, 0)),
        ],
        out_specs=pl.BlockSpec((tm, D_MODEL), lambda i: (i, 0)),
        out_shape=jax.ShapeDtypeStruct((rows, D_MODEL), F32),
        compiler_params=pltpu.CompilerParams(dimension_semantics=("arbitrary",)),
        name="outproj",
    )(x2d, osb, proj, yssd, sb_norm_w, w_out, final_w)


def _layer(x2d, meta, norm_w, w_in, conv_w, conv_b, dt_bias, a_log, d_skip, sb_norm_w,
           ssd_norm_w, w_out, nbatch, seq):
    nh = SSD_HEADS
    w_main = w_in.astype(BF16)
    w_dt = jnp.pad(w_in[:, D_MAIN:], ((0, 0), (0, LANES - nh))).astype(BF16)
    dtb = jnp.pad(dt_bias, (0, LANES - nh)).reshape(1, LANES)
    alog = jnp.pad(a_log, (0, LANES - nh)).reshape(1, LANES)
    norm_w = norm_w.reshape(1, D_MODEL)
    dsk_rep = jnp.repeat(d_skip, HEAD_DIM).reshape(1, SSD_WIDTH)
    e_mat = (jnp.arange(LANES)[:, None] == (jnp.arange(SSD_WIDTH)[None, :] // HEAD_DIM)).astype(BF16)
    conv_b = conv_b.reshape(1, SSD_XBC)
    ssd_norm_w = ssd_norm_w.reshape(1, SSD_WIDTH)

    proj_m, dt_m = _inproj(meta, norm_w, w_main, w_dt, dtb, tm=N_META)
    pad = CHUNK - N_META
    proj_mp = jnp.pad(proj_m, ((pad, 0), (0, 0)))
    dt_mp = jnp.pad(dt_m, ((pad, 0), (0, 0)))
    zeros_s = jnp.zeros((SSD_GROUPS, SSD_STATE, 512), F32)
    zeros_t = jnp.zeros((8, SSD_XBC), F32)
    _, s_meta, tail_meta = _ssd(proj_mp, dt_mp, zeros_s, zeros_t, conv_w, conv_b, alog, dsk_rep,
                                ssd_norm_w, e_mat, nbatch=1, ts=CHUNK, emit_state=True)
    kmeta = jnp.pad(proj_m[:, COL_K * 1024:(COL_K + 1) * 1024], ((0, LANES - N_META), (0, 0)))
    vmeta = jnp.pad(proj_m[:, COL_V * 1024:(COL_V + 1) * 1024], ((0, LANES - N_META), (0, 0)))

    proj, dt = _inproj(x2d, norm_w, w_main, w_dt, dtb, tm=512)
    (yssd,) = _ssd(proj, dt, s_meta, tail_meta, conv_w, conv_b, alog, dsk_rep, ssd_norm_w, e_mat,
                   nbatch=nbatch, ts=TQ, emit_state=False)
    osb = _attn(proj, kmeta, vmeta, nbatch=nbatch, seq=seq)
    return osb, proj, yssd


def kernel(x, meta_tokens, norm_w, w_in, conv_w, conv_b, dt_bias, a_log, d_skip, sb_norm_w,
           ssd_norm_w, w_out, final_norm_w):
    nbatch, seq, _ = x.shape
    assert norm_w.shape[0] == 1, "single-layer block"
    x2d = x.reshape(nbatch * seq, D_MODEL)
    osb, proj, yssd = _layer(x2d, meta_tokens, norm_w[0], w_in[0], conv_w[0], conv_b[0], dt_bias[0],
                             a_log[0], d_skip[0], sb_norm_w[0], ssd_norm_w[0], w_out[0], nbatch, seq)
    out = _outproj(x2d, osb, proj, yssd, sb_norm_w[0].reshape(1, SB_WIDTH), w_out[0].astype(BF16),
                   final_norm_w.reshape(1, D_MODEL))
    return out.reshape(nbatch, seq, D_MODEL)
```

```python
import functools
import math

import jax
import jax.numpy as jnp
from jax import lax
from jax.experimental import pallas as pl
from jax.experimental.pallas import tpu as pltpu

F32 = jnp.float32
BF16 = jnp.bfloat16

D_MODEL = 1024
N_META = 16
HEAD_DIM = 64
SB_WIDTH = 1024
SSD_WIDTH = 1024
SSD_HEADS = 16
SSD_GROUPS = 2
SSD_STATE = 128
SSD_CONV = 4
SSD_XBC = SSD_WIDTH + 2 * SSD_GROUPS * SSD_STATE
D_MAIN = 4 * SB_WIDTH + SSD_WIDTH + SSD_XBC
EPS = 1e-5

LANES = 128
PAIR = 2 * HEAD_DIM
CHUNK = 64
TQ = 256
NEG_BIG = -1e30
LOG2E = 1.4426950408889634
Q_SCALE = LOG2E / math.sqrt(HEAD_DIM)
C_SKIP = 160.0

COL_Q, COL_K, COL_V, COL_GATE, COL_Z, COL_XS = 0, 1, 2, 3, 4, 5
COL_B, COL_C = 24, 25
GATE0 = COL_GATE * 1024
XBC0 = COL_XS * 1024


def _softplus(x):
    return jnp.maximum(x, 0.0) + jnp.log(1.0 + jnp.exp(-jnp.abs(x)))


def _silu(x):
    return x / (1.0 + jnp.exp(-x))


def _split3(x):
    hi = x.astype(BF16)
    r1 = x - hi.astype(F32)
    mid = r1.astype(BF16)
    lo = (r1 - mid.astype(F32)).astype(BF16)
    return hi, mid, lo


def _dot(a, b):
    return jnp.dot(a, b, preferred_element_type=F32)


def _dot_nt(a, b):
    return lax.dot_general(a, b, (((1,), (1,)), ((), ())), preferred_element_type=F32)


def _dot_tn(a, b):
    return lax.dot_general(a, b, (((0,), (0,)), ((), ())), preferred_element_type=F32)


def _inproj_kernel(x_ref, nw_ref, w_ref, wdt_ref, dtb_ref, tail0_ref, cw_ref, cb_ref, *rest,
                   tm, sub, tn, tiles_per_seq, emit_tail):
    if emit_tail:
        proj_ref, dt_ref, tail_ref, ext_scr = rest
    else:
        proj_ref, dt_ref, ext_scr = rest

    @pl.when(pl.program_id(0) % tiles_per_seq == 0)
    def _():
        ext_scr[0:8, :] = tail0_ref[...]

    for r0 in range(0, tm, sub):
        rows = slice(r0, r0 + sub)
        x = x_ref[rows, :]
        u = x * lax.rsqrt(jnp.mean(x * x, axis=-1, keepdims=True) + EPS) * nw_ref[...]
        ub = u.astype(BF16)
        dt_ref[rows, :] = _softplus(_dot(ub, wdt_ref[...]) + dtb_ref[...])
        for c0 in range(XBC0, D_MAIN, tn):
            cc = slice(c0 - XBC0, c0 - XBC0 + tn)
            ext_scr[8:8 + sub, cc] = _dot(ub, w_ref[:, c0:c0 + tn])
            conv = cb_ref[:, cc] + cw_ref[0:1, cc] * ext_scr[5:5 + sub, cc]
            for k in range(1, SSD_CONV):
                conv = conv + cw_ref[k:k + 1, cc] * ext_scr[5 + k:5 + k + sub, cc]
            proj_ref[rows, c0:c0 + tn] = _silu(conv).astype(BF16)
            ext_scr[0:8, cc] = ext_scr[sub:sub + 8, cc]
        for c0 in list(range(GATE0, XBC0, tn)) + list(range(0, GATE0, tn)):
            acc = _dot(ub, w_ref[:, c0:c0 + tn])
            if c0 < SB_WIDTH:
                acc = acc * Q_SCALE
            elif c0 >= GATE0:
                acc = _silu(acc)
            proj_ref[rows, c0:c0 + tn] = acc.astype(BF16)
    if emit_tail:
        tail_ref[...] = ext_scr[0:8, :]


def _inproj(x2d, norm_w, w_all, w_dt, dt_bias, tail0, conv_w, conv_b, *, tm, rows_per_seq, emit_tail, tn=512):
    rows = x2d.shape[0]
    sub = min(tm, 256)
    assert SB_WIDTH % tn == 0 and GATE0 % tn == 0 and XBC0 % tn == 0 and D_MAIN % tn == 0
    assert rows_per_seq % tm == 0 and tm % sub == 0
    const = lambda i: (0, 0)
    out_specs = [pl.BlockSpec((tm, D_MAIN), lambda i: (i, 0)), pl.BlockSpec((tm, LANES), lambda i: (i, 0))]
    out_shape = [jax.ShapeDtypeStruct((rows, D_MAIN), BF16), jax.ShapeDtypeStruct((rows, LANES), F32)]
    if emit_tail:
        out_specs.append(pl.BlockSpec((8, SSD_XBC), const))
        out_shape.append(jax.ShapeDtypeStruct((8, SSD_XBC), F32))
    return pl.pallas_call(
        functools.partial(_inproj_kernel, tm=tm, sub=sub, tn=tn, tiles_per_seq=rows_per_seq // tm,
                          emit_tail=emit_tail),
        grid=(rows // tm,),
        in_specs=[
            pl.BlockSpec((tm, D_MODEL), lambda i: (i, 0)),
            pl.BlockSpec((1, D_MODEL), const),
            pl.BlockSpec((D_MODEL, D_MAIN), const, pipeline_mode=pl.Buffered(1)),
            pl.BlockSpec((D_MODEL, LANES), const),
            pl.BlockSpec((1, LANES), const),
            pl.BlockSpec((8, SSD_XBC), const),
            pl.BlockSpec((SSD_CONV, SSD_XBC), const),
            pl.BlockSpec((1, SSD_XBC), const),
        ],
        out_specs=out_specs,
        out_shape=out_shape,
        scratch_shapes=[pltpu.VMEM((sub + 8, SSD_XBC), F32)],
        compiler_params=pltpu.CompilerParams(dimension_semantics=("arbitrary",)),
        name="inproj",
    )(x2d, norm_w, w_all, w_dt, dt_bias, tail0, conv_w, conv_b)


def _ssd_kernel(zact_ref, xs_ref, b_ref, c_ref, dt_ref, s0_ref, alog_ref, dsk_ref, nw_ref, e_ref,
                *rest, ts, emit_state):
    if emit_state:
        y_ref, sfin_ref, s_scr, y_scr = rest
    else:
        y_ref, s_scr, y_scr = rest

    @pl.when(pl.program_id(1) == 0)
    def _():
        s_scr[...] = s0_ref[...]

    nchunk = ts // CHUNK
    a_row = -jnp.exp(alog_ref[...])
    li = lax.broadcasted_iota(jnp.int32, (ts, ts), 0)
    si = lax.broadcasted_iota(jnp.int32, (ts, ts), 1)
    t_incl = jnp.logical_and(si <= li, si // CHUNK == li // CHUNK).astype(BF16)
    l2 = lax.broadcasted_iota(jnp.int32, (CHUNK, LANES), 0)
    lane2 = lax.broadcasted_iota(jnp.int32, (CHUNK, LANES), 1)
    mask2 = (lane2 % CHUNK) <= l2
    low_half = lane2 < HEAD_DIM
    e_mat = e_ref[...]

    xs = xs_ref[...].astype(F32)
    dt = dt_ref[...]
    h3 = _split3(dt * a_row)
    acum = _dot(t_incl, h3[0]) + _dot(t_incl, h3[1]) + _dot(t_incl, h3[2])
    st3 = _split3(jnp.concatenate([dt, acum], axis=0))
    e2 = jnp.concatenate([e_mat, e_mat], axis=0)
    rep = _dot(jnp.concatenate([st3[0], st3[1]], axis=1), e2)
    dtr = rep[0:ts]
    acr = rep[ts:2 * ts] + _dot(st3[2][ts:2 * ts], e_mat)
    xdt = xs * dtr
    ebase = jnp.exp(acr)

    yoffs = []
    for c in range(nchunk):
        rows = slice(c * CHUNK, (c + 1) * CHUNK)
        atot = acr[(c + 1) * CHUNK - 1:(c + 1) * CHUNK, :]
        xw = (xdt[rows] * jnp.exp(atot - acr[rows])).astype(BF16)
        etot = jnp.exp(atot)
        for g in range(SSD_GROUPS):
            gs = slice(g * 512, (g + 1) * 512)
            ns = slice(g * SSD_STATE, (g + 1) * SSD_STATE)
            sg = s_scr[g]
            yoffs.append(_dot(c_ref[rows, ns], sg.astype(BF16)) * ebase[rows, gs])
            s_scr[g] = sg * etot[:, gs] + _dot_tn(b_ref[rows, ns], xw[:, gs])

    for c in range(nchunk):
        rows = slice(c * CHUNK, (c + 1) * CHUNK)
        ac = acum[rows]
        wt = jnp.concatenate([ac, pltpu.roll(ac, LANES - 1, axis=1)], axis=0).T
        for g in range(SSD_GROUPS):
            ns = slice(g * SSD_STATE, (g + 1) * SSD_STATE)
            bg = b_ref[rows, ns]
            cb2 = _dot_nt(c_ref[rows, ns], jnp.concatenate([bg, bg], axis=0))
            yoff = yoffs[c * SSD_GROUPS + g]
            for pp in range(4):
                p = 4 * g + pp
                cols = slice(p * PAIR, (p + 1) * PAIR)
                seg = acr[rows, cols] - wt[2 * p:2 * p + 1, :]
                m2 = (cb2 * jnp.exp(jnp.where(mask2, seg, NEG_BIG))).astype(BF16)
                x2 = xdt[rows, cols].astype(BF16)
                zero = jnp.zeros_like(x2)
                xblk = jnp.concatenate([jnp.where(low_half, x2, zero),
                                        jnp.where(low_half, zero, x2)], axis=0)
                y_scr[rows, cols] = (_dot(m2, xblk) + yoff[:, pp * PAIR:(pp + 1) * PAIR]
                                     + xs[rows, cols] * dsk_ref[:, cols])

    yg = y_scr[...] * zact_ref[...].astype(F32)
    y_ref[...] = (yg * lax.rsqrt(jnp.mean(yg * yg, axis=-1, keepdims=True) + EPS)
                  * nw_ref[...]).astype(BF16)
    if emit_state:
        sfin_ref[...] = s_scr[...]


def _ssd(proj, dt, s0, a_log, dsk_rep, norm_w, e_mat, *, nbatch, ts, emit_state):
    rows = proj.shape[0]
    nsteps = rows // (nbatch * ts)
    rb = lambda b, s: b * nsteps + s
    const = lambda b, s: (0, 0)
    in_specs = [
        pl.BlockSpec((ts, 1024), lambda b, s: (rb(b, s), COL_Z)),
        pl.BlockSpec((ts, 1024), lambda b, s: (rb(b, s), COL_XS)),
        pl.BlockSpec((ts, 256), lambda b, s: (rb(b, s), COL_B)),
        pl.BlockSpec((ts, 256), lambda b, s: (rb(b, s), COL_C)),
        pl.BlockSpec((ts, LANES), lambda b, s: (rb(b, s), 0)),
        pl.BlockSpec((SSD_GROUPS, SSD_STATE, 512), lambda b, s: (0, 0, 0)),
        pl.BlockSpec((1, LANES), const),
        pl.BlockSpec((1, SSD_WIDTH), const),
        pl.BlockSpec((1, SSD_WIDTH), const),
        pl.BlockSpec((LANES, SSD_WIDTH), const),
    ]
    out_specs = [pl.BlockSpec((ts, SSD_WIDTH), lambda b, s: (rb(b, s), 0))]
    out_shape = [jax.ShapeDtypeStruct((rows, SSD_WIDTH), BF16)]
    if emit_state:
        out_specs.append(pl.BlockSpec((SSD_GROUPS, SSD_STATE, 512), lambda b, s: (0, 0, 0)))
        out_shape.append(jax.ShapeDtypeStruct((SSD_GROUPS, SSD_STATE, 512), F32))
    return pl.pallas_call(
        functools.partial(_ssd_kernel, ts=ts, emit_state=emit_state),
        grid=(nbatch, nsteps),
        in_specs=in_specs,
        out_specs=out_specs,
        out_shape=out_shape,
        scratch_shapes=[
            pltpu.VMEM((SSD_GROUPS, SSD_STATE, 512), F32),
            pltpu.VMEM((ts, SSD_WIDTH), F32),
        ],
        compiler_params=pltpu.CompilerParams(dimension_semantics=("arbitrary", "arbitrary")),
        name="ssd_meta" if emit_state else "ssd",
    )(proj, proj, proj, proj, dt, s0, a_log, dsk_rep, norm_w, e_mat)


def _sb_tiles(items, tmat):
    n = len(items)
    ys = [_dot_nt(it[0], it[1]) for it in items]
    sps = []
    for y, it in zip(ys, items):
        sp = jnp.maximum(y, 0.0) + jnp.log(1.0 + jnp.exp2(-jnp.abs(y))) * LOG2E
        if it[3] is not None:
            sp = jnp.where(it[3], sp, 0.0)
        sps.append(sp)
    excls = [_dot(sp.astype(BF16), tmat[0:sp.shape[1], 0:sp.shape[1]]) for sp in sps]
    rss = [excls[i][:, 0:1] + sps[i][:, 0:1] for i in range(n)]
    c_in, c_out = [], []
    for i, it in enumerate(items):
        c = c_out[it[6]] if it[6] is not None else it[4]
        c_in.append(c)
        c_out.append(rss[i] if c is None else c + rss[i])
    ws = []
    for i, it in enumerate(items):
        e = ys[i] - sps[i] - excls[i]
        if c_in[i] is not None:
            e = e - c_in[i]
        w = jnp.exp2(e)
        if it[3] is not None:
            w = jnp.where(it[3], w, 0.0)
        ws.append(w.astype(BF16))
    pvs = [_dot(ws[i], items[i][2]) for i in range(n)]
    a_out = []
    for i, it in enumerate(items):
        a = a_out[it[6]] if it[6] is not None else it[5]
        a_out.append(pvs[i] if a is None else a + pvs[i])
    return c_out, a_out


def _attn_kernel(q_ref, k_ref, v_ref, km_ref, vm_ref, o_ref, c_scr, acc_scr, *, nq, npair):
    nchain = 2 * npair
    lane = lax.broadcasted_iota(jnp.int32, (TQ, PAIR), 1)
    low = lane < HEAD_DIM
    row = lax.broadcasted_iota(jnp.int32, (TQ, TQ), 0)
    col = lax.broadcasted_iota(jnp.int32, (TQ, TQ), 1)
    tmat = (row > col).astype(BF16)
    diag_mask = col < row
    meta_mask = lax.broadcasted_iota(jnp.int32, (TQ, LANES), 1) < N_META

    def cols(p):
        return slice(p * PAIR, (p + 1) * PAIR)

    def load_q(r0):
        qs = []
        for p in range(npair):
            q2 = q_ref[pl.ds(r0, TQ), cols(p)]
            zero = jnp.zeros_like(q2)
            qs += [jnp.where(low, q2, zero), jnp.where(low, zero, q2)]
        return qs

    def kv_tile(qs, k0, mask, state):
        items = []
        for i in range(nchain):
            kb = k_ref[pl.ds(k0, TQ), cols(i // 2)]
            vb = v_ref[pl.ds(k0, TQ), cols(i // 2)]
            if state == "scratch":
                items.append((qs[i], kb, vb, mask, c_scr[i], acc_scr[i], None))
            elif state == "fresh":
                items.append((qs[i], kb, vb, mask, None, None, None))
            else:
                items.append((qs[i], kb, vb, mask, None, None, state + i))
        return items

    def commit(cs, accs):
        for i in range(nchain):
            c_scr[i] = cs[i]
            acc_scr[i] = accs[i]

    def unfinished():
        m = c_scr[0]
        for i in range(1, nchain):
            m = jnp.minimum(m, c_scr[i])
        return (jnp.min(m) < C_SKIP).astype(jnp.int32)

    def meta_tile(qs):
        items = [(qs[i], km_ref[:, cols(i // 2)], vm_ref[:, cols(i // 2)], meta_mask,
                  c_scr[i], acc_scr[i], None) for i in range(nchain)]
        commit(*_sb_tiles(items, tmat))

    def write_out(r0):
        for p in range(npair):
            o_ref[pl.ds(r0, TQ), cols(p)] = jnp.where(low, acc_scr[2 * p], acc_scr[2 * p + 1]).astype(BF16)

    qs0 = load_q(0)
    commit(*_sb_tiles(kv_tile(qs0, 0, diag_mask, "fresh"), tmat))
    meta_tile(qs0)
    write_out(0)

    def qblock(qi, carry):
        r0 = pl.multiple_of(qi * TQ, TQ)
        qs = load_q(r0)
        items = (kv_tile(qs, r0, diag_mask, "fresh")
                 + kv_tile(qs, pl.multiple_of(r0 - TQ, TQ), None, 0))
        cs, accs = _sb_tiles(items, tmat)
        commit(cs[nchain:], accs[nchain:])

        def cond(st):
            return jnp.logical_and(st[0] < qi, st[1] > 0)

        def body(st):
            k0 = pl.multiple_of((qi - 1 - st[0]) * TQ, TQ)
            commit(*_sb_tiles(kv_tile(qs, k0, None, "scratch"), tmat))
            return st[0] + 1, unfinished()

        _, more = lax.while_loop(cond, body, (jnp.int32(1), unfinished()))

        @pl.when(more > 0)
        def _():
            meta_tile(qs)

        write_out(r0)
        return carry

    lax.fori_loop(1, nq, qblock, 0)


def _attn(proj, kmeta, vmeta, *, nbatch, seq, npair=2):
    width = npair * PAIR
    nblk = SB_WIDTH // width
    return pl.pallas_call(
        functools.partial(_attn_kernel, nq=seq // TQ, npair=npair),
        grid=(nbatch, nblk),
        in_specs=[
            pl.BlockSpec((seq, width), lambda b, p: (b, COL_Q * nblk + p)),
            pl.BlockSpec((seq, width), lambda b, p: (b, COL_K * nblk + p)),
            pl.BlockSpec((seq, width), lambda b, p: (b, COL_V * nblk + p)),
            pl.BlockSpec((LANES, width), lambda b, p: (0, p)),
            pl.BlockSpec((LANES, width), lambda b, p: (0, p)),
        ],
        out_specs=pl.BlockSpec((seq, width), lambda b, p: (b, p)),
        out_shape=jax.ShapeDtypeStruct((nbatch * seq, SB_WIDTH), BF16),
        scratch_shapes=[pltpu.VMEM((2 * npair, TQ, 1), F32), pltpu.VMEM((2 * npair, TQ, PAIR), F32)],
        compiler_params=pltpu.CompilerParams(dimension_semantics=("arbitrary", "arbitrary")),
        name="sb_attn",
    )(proj, proj, proj, kmeta, vmeta)


def _outproj_kernel(x_ref, osb_ref, gate_ref, yssd_ref, sbw_ref, wo_ref, fw_ref, o_ref, *, tm, sub):
    for r0 in range(0, tm, sub):
        rows = slice(r0, r0 + sub)
        ys = osb_ref[rows, :].astype(F32) * gate_ref[rows, :].astype(F32)
        ysb = ys * lax.rsqrt(jnp.mean(ys * ys, axis=-1, keepdims=True) + EPS) * sbw_ref[...]
        h = (x_ref[rows, :] + _dot(ysb.astype(BF16), wo_ref[0:SB_WIDTH, :])
             + _dot(yssd_ref[rows, :], wo_ref[SB_WIDTH:SB_WIDTH + SSD_WIDTH, :]))
        o_ref[rows, :] = h * lax.rsqrt(jnp.mean(h * h, axis=-1, keepdims=True) + EPS) * fw_ref[...]


def _outproj(x2d, osb, proj, yssd, sb_norm_w, w_out, final_w, *, tm=512):
    rows = x2d.shape[0]
    return pl.pallas_call(
        functools.partial(_outproj_kernel, tm=tm, sub=256),
        grid=(rows // tm,),
        in_specs=[
            pl.BlockSpec((tm, D_MODEL), lambda i: (i, 0)),
            pl.BlockSpec((tm, SB_WIDTH), lambda i: (i, 0)),
            pl.BlockSpec((tm, SB_WIDTH), lambda i: (i, COL_GATE)),
            pl.BlockSpec((tm, SSD_WIDTH), lambda i: (i, 0)),
            pl.BlockSpec((1, SB_WIDTH), lambda i: (0, 0)),
            pl.BlockSpec((SB_WIDTH + SSD_WIDTH, D_MODEL), lambda i: (0, 0)),
            pl.BlockSpec((1, D_MODEL), lambda i: (0, 0)),
        ],
        out_specs=pl.BlockSpec((tm, D_MODEL), lambda i: (i, 0)),
        out_shape=jax.ShapeDtypeStruct((rows, D_MODEL), F32),
        compiler_params=pltpu.CompilerParams(dimension_semantics=("arbitrary",)),
        name="outproj",
    )(x2d, osb, proj, yssd, sb_norm_w, w_out, final_w)


def _layer(x2d, meta, norm_w, w_in, conv_w, conv_b, dt_bias, a_log, d_skip, sb_norm_w,
           ssd_norm_w, w_out, nbatch, seq):
    nh = SSD_HEADS
    w_main = w_in.astype(BF16)
    w_dt = jnp.pad(w_in[:, D_MAIN:], ((0, 0), (0, LANES - nh))).astype(BF16)
    dtb = jnp.pad(dt_bias, (0, LANES - nh)).reshape(1, LANES)
    alog = jnp.pad(a_log, (0, LANES - nh)).reshape(1, LANES)
    norm_w = norm_w.reshape(1, D_MODEL)
    dsk_rep = jnp.repeat(d_skip, HEAD_DIM).reshape(1, SSD_WIDTH)
    e_mat = (jnp.arange(LANES)[:, None] == (jnp.arange(SSD_WIDTH)[None, :] // HEAD_DIM)).astype(BF16)
    conv_b = conv_b.reshape(1, SSD_XBC)
    ssd_norm_w = ssd_norm_w.reshape(1, SSD_WIDTH)

    zeros_t = jnp.zeros((8, SSD_XBC), F32)
    proj_m, dt_m, tail_meta = _inproj(meta, norm_w, w_main, w_dt, dtb, zeros_t, conv_w, conv_b,
                                      tm=N_META, rows_per_seq=N_META, emit_tail=True)
    pad = CHUNK - N_META
    proj_mp = jnp.pad(proj_m, ((pad, 0), (0, 0)))
    dt_mp = jnp.pad(dt_m, ((pad, 0), (0, 0)))
    zeros_s = jnp.zeros((SSD_GROUPS, SSD_STATE, 512), F32)
    _, s_meta = _ssd(proj_mp, dt_mp, zeros_s, alog, dsk_rep, ssd_norm_w, e_mat,
                     nbatch=1, ts=CHUNK, emit_state=True)
    kmeta = jnp.pad(proj_m[:, COL_K * 1024:(COL_K + 1) * 1024], ((0, LANES - N_META), (0, 0)))
    vmeta = jnp.pad(proj_m[:, COL_V * 1024:(COL_V + 1) * 1024], ((0, LANES - N_META), (0, 0)))

    proj, dt = _inproj(x2d, norm_w, w_main, w_dt, dtb, tail_meta, conv_w, conv_b,
                       tm=512, rows_per_seq=seq, emit_tail=False)
    (yssd,) = _ssd(proj, dt, s_meta, alog, dsk_rep, ssd_norm_w, e_mat,
                   nbatch=nbatch, ts=TQ, emit_state=False)
    osb = _attn(proj, kmeta, vmeta, nbatch=nbatch, seq=seq)
    return osb, proj, yssd


def kernel(x, meta_tokens, norm_w, w_in, conv_w, conv_b, dt_bias, a_log, d_skip, sb_norm_w,
           ssd_norm_w, w_out, final_norm_w):
    nbatch, seq, _ = x.shape
    assert norm_w.shape[0] == 1, "single-layer block"
    x2d = x.reshape(nbatch * seq, D_MODEL)
    osb, proj, yssd = _layer(x2d, meta_tokens, norm_w[0], w_in[0], conv_w[0], conv_b[0], dt_bias[0],
                             a_log[0], d_skip[0], sb_norm_w[0], ssd_norm_w[0], w_out[0], nbatch, seq)
    out = _outproj(x2d, osb, proj, yssd, sb_norm_w[0].reshape(1, SB_WIDTH), w_out[0].astype(BF16),
                   final_norm_w.reshape(1, D_MODEL))
    return out.reshape(nbatch, seq, D_MODEL)
```

```python
import functools
import math

import jax
import jax.numpy as jnp
from jax import lax
from jax.experimental import pallas as pl
from jax.experimental.pallas import tpu as pltpu

F32 = jnp.float32
BF16 = jnp.bfloat16

D_MODEL = 1024
N_META = 16
HEAD_DIM = 64
SB_WIDTH = 1024
SSD_WIDTH = 1024
SSD_HEADS = 16
SSD_GROUPS = 2
SSD_STATE = 128
SSD_CONV = 4
SSD_XBC = SSD_WIDTH + 2 * SSD_GROUPS * SSD_STATE
D_MAIN = 4 * SB_WIDTH + SSD_WIDTH + SSD_XBC
EPS = 1e-5

LANES = 128
PAIR = 2 * HEAD_DIM
CHUNK = 64
TQ = 256
NEG_BIG = -1e30
Q_SCALE = 1.0 / math.sqrt(HEAD_DIM)
C_SKIP = 111.0

COL_Q, COL_K, COL_V, COL_GATE, COL_Z, COL_XS = 0, 1, 2, 3, 4, 5
COL_B, COL_C = 24, 25
GATE0 = COL_GATE * 1024
XBC0 = COL_XS * 1024


def _softplus(x):
    return jnp.maximum(x, 0.0) + jnp.log(1.0 + jnp.exp(-jnp.abs(x)))


def _silu(x):
    return x / (1.0 + jnp.exp(-x))


def _split3(x):
    hi = x.astype(BF16)
    r1 = x - hi.astype(F32)
    mid = r1.astype(BF16)
    lo = (r1 - mid.astype(F32)).astype(BF16)
    return hi, mid, lo


def _dot(a, b):
    return jnp.dot(a, b, preferred_element_type=F32)


def _dot_nt(a, b):
    return lax.dot_general(a, b, (((1,), (1,)), ((), ())), preferred_element_type=F32)


def _dot_tn(a, b):
    return lax.dot_general(a, b, (((0,), (0,)), ((), ())), preferred_element_type=F32)


def _inproj_kernel(x_ref, nw_ref, w_ref, wdt_ref, dtb_ref, tail0_ref, cw_ref, cb_ref, *rest,
                   tm, sub, tn, tiles_per_seq, emit_tail):
    if emit_tail:
        proj_ref, dt_ref, tail_ref, ext_scr = rest
    else:
        proj_ref, dt_ref, ext_scr = rest

    @pl.when(pl.program_id(0) % tiles_per_seq == 0)
    def _():
        ext_scr[0:8, :] = tail0_ref[...]

    for r0 in range(0, tm, sub):
        rows = slice(r0, r0 + sub)
        x = x_ref[rows, :]
        u = x * lax.rsqrt(jnp.mean(x * x, axis=-1, keepdims=True) + EPS) * nw_ref[...]
        ub = u.astype(BF16)
        dt_ref[rows, :] = _softplus(_dot(ub, wdt_ref[...]) + dtb_ref[...])
        for c0 in range(XBC0, D_MAIN, tn):
            cc = slice(c0 - XBC0, c0 - XBC0 + tn)
            ext_scr[8:8 + sub, cc] = _dot(ub, w_ref[:, c0:c0 + tn])
            conv = cb_ref[:, cc] + cw_ref[0:1, cc] * ext_scr[5:5 + sub, cc]
            for k in range(1, SSD_CONV):
                conv = conv + cw_ref[k:k + 1, cc] * ext_scr[5 + k:5 + k + sub, cc]
            proj_ref[rows, c0:c0 + tn] = _silu(conv).astype(BF16)
            ext_scr[0:8, cc] = ext_scr[sub:sub + 8, cc]
        for c0 in list(range(GATE0, XBC0, tn)) + list(range(0, GATE0, tn)):
            acc = _dot(ub, w_ref[:, c0:c0 + tn])
            if c0 < SB_WIDTH:
                acc = acc * Q_SCALE
            elif c0 >= GATE0:
                acc = _silu(acc)
            proj_ref[rows, c0:c0 + tn] = acc.astype(BF16)
    if emit_tail:
        tail_ref[...] = ext_scr[0:8, :]


def _inproj(x2d, norm_w, w_all, w_dt, dt_bias, tail0, conv_w, conv_b, *, tm, rows_per_seq, emit_tail, tn=512):
    rows = x2d.shape[0]
    sub = min(tm, 256)
    assert SB_WIDTH % tn == 0 and GATE0 % tn == 0 and XBC0 % tn == 0 and D_MAIN % tn == 0
    assert rows_per_seq % tm == 0 and tm % sub == 0
    const = lambda i: (0, 0)
    out_specs = [pl.BlockSpec((tm, D_MAIN), lambda i: (i, 0)), pl.BlockSpec((tm, LANES), lambda i: (i, 0))]
    out_shape = [jax.ShapeDtypeStruct((rows, D_MAIN), BF16), jax.ShapeDtypeStruct((rows, LANES), F32)]
    if emit_tail:
        out_specs.append(pl.BlockSpec((8, SSD_XBC), const))
        out_shape.append(jax.ShapeDtypeStruct((8, SSD_XBC), F32))
    return pl.pallas_call(
        functools.partial(_inproj_kernel, tm=tm, sub=sub, tn=tn, tiles_per_seq=rows_per_seq // tm,
                          emit_tail=emit_tail),
        grid=(rows // tm,),
        in_specs=[
            pl.BlockSpec((tm, D_MODEL), lambda i: (i, 0)),
            pl.BlockSpec((1, D_MODEL), const),
            pl.BlockSpec((D_MODEL, D_MAIN), const, pipeline_mode=pl.Buffered(1)),
            pl.BlockSpec((D_MODEL, LANES), const),
            pl.BlockSpec((1, LANES), const),
            pl.BlockSpec((8, SSD_XBC), const),
            pl.BlockSpec((SSD_CONV, SSD_XBC), const),
            pl.BlockSpec((1, SSD_XBC), const),
        ],
        out_specs=out_specs,
        out_shape=out_shape,
        scratch_shapes=[pltpu.VMEM((sub + 8, SSD_XBC), F32)],
        compiler_params=pltpu.CompilerParams(dimension_semantics=("arbitrary",)),
        name="inproj",
    )(x2d, norm_w, w_all, w_dt, dt_bias, tail0, conv_w, conv_b)


def _ssd_kernel(zact_ref, xs_ref, b_ref, c_ref, dt_ref, s0_ref, alog_ref, dsk_ref, nw_ref, e_ref,
                *rest, ts, emit_state):
    if emit_state:
        y_ref, sfin_ref, s_scr, y_scr = rest
    else:
        y_ref, s_scr, y_scr = rest

    @pl.when(pl.program_id(1) == 0)
    def _():
        s_scr[...] = s0_ref[...]

    nchunk = ts // CHUNK
    a_row = -jnp.exp(alog_ref[...])
    li = lax.broadcasted_iota(jnp.int32, (ts, ts), 0)
    si = lax.broadcasted_iota(jnp.int32, (ts, ts), 1)
    t_incl = jnp.logical_and(si <= li, si // CHUNK == li // CHUNK).astype(BF16)
    l2 = lax.broadcasted_iota(jnp.int32, (CHUNK, LANES), 0)
    lane2 = lax.broadcasted_iota(jnp.int32, (CHUNK, LANES), 1)
    mask2 = (lane2 % CHUNK) <= l2
    low_half = lane2 < HEAD_DIM
    e_mat = e_ref[...]

    xs = xs_ref[...].astype(F32)
    dt = dt_ref[...]
    h3 = _split3(dt * a_row)
    acum = _dot(t_incl, h3[0]) + _dot(t_incl, h3[1]) + _dot(t_incl, h3[2])
    st3 = _split3(jnp.concatenate([dt, acum], axis=0))
    e2 = jnp.concatenate([e_mat, e_mat], axis=0)
    rep = _dot(jnp.concatenate([st3[0], st3[1]], axis=1), e2)
    dtr = rep[0:ts]
    acr = rep[ts:2 * ts] + _dot(st3[2][ts:2 * ts], e_mat)
    xdt = xs * dtr
    ebase = jnp.exp(acr)

    chunks = [slice(c * CHUNK, (c + 1) * CHUNK) for c in range(nchunk)]

    yoffs = []
    for c, r in enumerate(chunks):
        atot = acr[(c + 1) * CHUNK - 1:(c + 1) * CHUNK, :]
        xw = (xdt[r] * jnp.exp(atot - acr[r])).astype(BF16)
        etot = jnp.exp(atot)
        for g in range(SSD_GROUPS):
            gs = slice(g * 512, (g + 1) * 512)
            ns = slice(g * SSD_STATE, (g + 1) * SSD_STATE)
            sg = s_scr[g]
            yoffs.append(_dot(c_ref[r, ns], sg.astype(BF16)) * ebase[r, gs])
            s_scr[g] = sg * etot[:, gs] + _dot_tn(b_ref[r, ns], xw[:, gs])

    wts = [jnp.concatenate([acum[r], pltpu.roll(acum[r], LANES - 1, axis=1)], axis=0).T for r in chunks]
    cb2s = []
    for r in chunks:
        for g in range(SSD_GROUPS):
            ns = slice(g * SSD_STATE, (g + 1) * SSD_STATE)
            bg = b_ref[r, ns]
            cb2s.append(_dot_nt(c_ref[r, ns], jnp.concatenate([bg, bg], axis=0)))
    npairs = SSD_HEADS // 2
    m2s, xblks = [], []
    for c, r in enumerate(chunks):
        for p in range(npairs):
            cols = slice(p * PAIR, (p + 1) * PAIR)
            seg = acr[r, cols] - wts[c][2 * p:2 * p + 1, :]
            m2s.append((cb2s[c * SSD_GROUPS + p // 4] * jnp.exp(jnp.where(mask2, seg, NEG_BIG))).astype(BF16))
            x2 = xdt[r, cols].astype(BF16)
            zero = jnp.zeros_like(x2)
            xblks.append(jnp.concatenate([jnp.where(low_half, x2, zero),
                                          jnp.where(low_half, zero, x2)], axis=0))
    ydiags = [_dot(m2, xblk) for m2, xblk in zip(m2s, xblks)]
    for c, r in enumerate(chunks):
        for p in range(npairs):
            cols = slice(p * PAIR, (p + 1) * PAIR)
            yoff = yoffs[c * SSD_GROUPS + p // 4]
            y_scr[r, cols] = (ydiags[c * npairs + p] + yoff[:, (p % 4) * PAIR:(p % 4 + 1) * PAIR]
                              + xs[r, cols] * dsk_ref[:, cols])

    yg = y_scr[...] * zact_ref[...].astype(F32)
    y_ref[...] = (yg * lax.rsqrt(jnp.mean(yg * yg, axis=-1, keepdims=True) + EPS)
                  * nw_ref[...]).astype(BF16)
    if emit_state:
        sfin_ref[...] = s_scr[...]


def _ssd(proj, dt, s0, a_log, dsk_rep, norm_w, e_mat, *, nbatch, ts, emit_state):
    rows = proj.shape[0]
    nsteps = rows // (nbatch * ts)
    rb = lambda b, s: b * nsteps + s
    const = lambda b, s: (0, 0)
    in_specs = [
        pl.BlockSpec((ts, 1024), lambda b, s: (rb(b, s), COL_Z)),
        pl.BlockSpec((ts, 1024), lambda b, s: (rb(b, s), COL_XS)),
        pl.BlockSpec((ts, 256), lambda b, s: (rb(b, s), COL_B)),
        pl.BlockSpec((ts, 256), lambda b, s: (rb(b, s), COL_C)),
        pl.BlockSpec((ts, LANES), lambda b, s: (rb(b, s), 0)),
        pl.BlockSpec((SSD_GROUPS, SSD_STATE, 512), lambda b, s: (0, 0, 0)),
        pl.BlockSpec((1, LANES), const),
        pl.BlockSpec((1, SSD_WIDTH), const),
        pl.BlockSpec((1, SSD_WIDTH), const),
        pl.BlockSpec((LANES, SSD_WIDTH), const),
    ]
    out_specs = [pl.BlockSpec((ts, SSD_WIDTH), lambda b, s: (rb(b, s), 0))]
    out_shape = [jax.ShapeDtypeStruct((rows, SSD_WIDTH), BF16)]
    if emit_state:
        out_specs.append(pl.BlockSpec((SSD_GROUPS, SSD_STATE, 512), lambda b, s: (0, 0, 0)))
        out_shape.append(jax.ShapeDtypeStruct((SSD_GROUPS, SSD_STATE, 512), F32))
    return pl.pallas_call(
        functools.partial(_ssd_kernel, ts=ts, emit_state=emit_state),
        grid=(nbatch, nsteps),
        in_specs=in_specs,
        out_specs=out_specs,
        out_shape=out_shape,
        scratch_shapes=[
            pltpu.VMEM((SSD_GROUPS, SSD_STATE, 512), F32),
            pltpu.VMEM((ts, SSD_WIDTH), F32),
        ],
        compiler_params=pltpu.CompilerParams(dimension_semantics=("arbitrary", "arbitrary")),
        name="ssd_meta" if emit_state else "ssd",
    )(proj, proj, proj, proj, dt, s0, a_log, dsk_rep, norm_w, e_mat)


def _sb_tiles(items, tmat):
    n = len(items)
    ys = [_dot_nt(it[0], it[1]).astype(BF16) for it in items]
    sps, owns = [], []
    for y, it in zip(ys, items):
        lg = jnp.log(1.0 + jnp.exp(-jnp.abs(y)))
        sp = jnp.maximum(y, 0.0) + lg
        owns.append(jnp.minimum(y, 0.0) - lg)
        if it[3] is not None:
            sp = jnp.where(it[3], sp, jnp.zeros_like(sp))
        sps.append(sp)
    excls = [_dot(sp, tmat[0:sp.shape[1], 0:sp.shape[1]]) for sp in sps]
    rss = [excls[i][:, 0:1] + sps[i][:, 0:1].astype(F32) for i in range(n)]
    c_in, c_out = [], []
    for i, it in enumerate(items):
        c = c_out[it[6]] if it[6] is not None else it[4]
        c_in.append(c)
        c_out.append(rss[i] if c is None else c + rss[i])
    ws = []
    for i, it in enumerate(items):
        t = excls[i] if c_in[i] is None else excls[i] + c_in[i]
        w = jnp.exp(owns[i] - t.astype(BF16))
        if it[3] is not None:
            w = jnp.where(it[3], w, jnp.zeros_like(w))
        ws.append(w)
    pvs = [_dot(ws[i], items[i][2]) for i in range(n)]
    a_out = []
    for i, it in enumerate(items):
        a = a_out[it[6]] if it[6] is not None else it[5]
        a_out.append(pvs[i] if a is None else a + pvs[i])
    return c_out, a_out


def _attn_kernel(q_ref, k_ref, v_ref, km_ref, vm_ref, o_ref, c_scr, acc_scr, *, nq, npair):
    nchain = 2 * npair
    lane = lax.broadcasted_iota(jnp.int32, (TQ, PAIR), 1)
    low = lane < HEAD_DIM
    row = lax.broadcasted_iota(jnp.int32, (TQ, TQ), 0)
    col = lax.broadcasted_iota(jnp.int32, (TQ, TQ), 1)
    tmat = (row > col).astype(BF16)
    diag_mask = col < row
    meta_mask = lax.broadcasted_iota(jnp.int32, (TQ, LANES), 1) < N_META

    def cols(p):
        return slice(p * PAIR, (p + 1) * PAIR)

    def load_q(r0):
        qs = []
        for p in range(npair):
            q2 = q_ref[pl.ds(r0, TQ), cols(p)]
            zero = jnp.zeros_like(q2)
            qs += [jnp.where(low, q2, zero), jnp.where(low, zero, q2)]
        return qs

    def kv_tile(qs, k0, mask, state):
        items = []
        for i in range(nchain):
            kb = k_ref[pl.ds(k0, TQ), cols(i // 2)]
            vb = v_ref[pl.ds(k0, TQ), cols(i // 2)]
            if state == "scratch":
                items.append((qs[i], kb, vb, mask, c_scr[i], acc_scr[i], None))
            elif state == "fresh":
                items.append((qs[i], kb, vb, mask, None, None, None))
            else:
                items.append((qs[i], kb, vb, mask, None, None, state + i))
        return items

    def commit(cs, accs):
        for i in range(nchain):
            c_scr[i] = cs[i]
            acc_scr[i] = accs[i]

    def unfinished():
        m = c_scr[0]
        for i in range(1, nchain):
            m = jnp.minimum(m, c_scr[i])
        return (jnp.min(m) < C_SKIP).astype(jnp.int32)

    def meta_tile(qs):
        items = [(qs[i], km_ref[:, cols(i // 2)], vm_ref[:, cols(i // 2)], meta_mask,
                  c_scr[i], acc_scr[i], None) for i in range(nchain)]
        commit(*_sb_tiles(items, tmat))

    def write_out(r0):
        for p in range(npair):
            o_ref[pl.ds(r0, TQ), cols(p)] = jnp.where(low, acc_scr[2 * p], acc_scr[2 * p + 1]).astype(BF16)

    qs0 = load_q(0)
    commit(*_sb_tiles(kv_tile(qs0, 0, diag_mask, "fresh"), tmat))
    meta_tile(qs0)
    write_out(0)

    def qblock(qi, carry):
        r0 = pl.multiple_of(qi * TQ, TQ)
        qs = load_q(r0)
        items = (kv_tile(qs, r0, diag_mask, "fresh")
                 + kv_tile(qs, pl.multiple_of(r0 - TQ, TQ), None, 0))
        cs, accs = _sb_tiles(items, tmat)
        commit(cs[nchain:], accs[nchain:])

        def cond(st):
            return jnp.logical_and(st[0] < qi, st[1] > 0)

        def body(st):
            k0 = pl.multiple_of((qi - 1 - st[0]) * TQ, TQ)
            commit(*_sb_tiles(kv_tile(qs, k0, None, "scratch"), tmat))
            return st[0] + 1, unfinished()

        _, more = lax.while_loop(cond, body, (jnp.int32(1), unfinished()))

        @pl.when(more > 0)
        def _():
            meta_tile(qs)

        write_out(r0)
        return carry

    lax.fori_loop(1, nq, qblock, 0)


def _attn(proj, kmeta, vmeta, *, nbatch, seq, npair=2):
    width = npair * PAIR
    nblk = SB_WIDTH // width
    return pl.pallas_call(
        functools.partial(_attn_kernel, nq=seq // TQ, npair=npair),
        grid=(nbatch, nblk),
        in_specs=[
            pl.BlockSpec((seq, width), lambda b, p: (b, COL_Q * nblk + p)),
            pl.BlockSpec((seq, width), lambda b, p: (b, COL_K * nblk + p)),
            pl.BlockSpec((seq, width), lambda b, p: (b, COL_V * nblk + p)),
            pl.BlockSpec((LANES, width), lambda b, p: (0, p)),
            pl.BlockSpec((LANES, width), lambda b, p: (0, p)),
        ],
        out_specs=pl.BlockSpec((seq, width), lambda b, p: (b, p)),
        out_shape=jax.ShapeDtypeStruct((nbatch * seq, SB_WIDTH), BF16),
        scratch_shapes=[pltpu.VMEM((2 * npair, TQ, 1), F32), pltpu.VMEM((2 * npair, TQ, PAIR), F32)],
        compiler_params=pltpu.CompilerParams(dimension_semantics=("arbitrary", "arbitrary")),
        name="sb_attn",
    )(proj, proj, proj, kmeta, vmeta)


def _outproj_kernel(x_ref, osb_ref, gate_ref, yssd_ref, sbw_ref, wo_ref, fw_ref, o_ref, *, tm, sub):
    for r0 in range(0, tm, sub):
        rows = slice(r0, r0 + sub)
        ys = osb_ref[rows, :].astype(F32) * gate_ref[rows, :].astype(F32)
        ysb = ys * lax.rsqrt(jnp.mean(ys * ys, axis=-1, keepdims=True) + EPS) * sbw_ref[...]
        h = (x_ref[rows, :] + _dot(ysb.astype(BF16), wo_ref[0:SB_WIDTH, :])
             + _dot(yssd_ref[rows, :], wo_ref[SB_WIDTH:SB_WIDTH + SSD_WIDTH, :]))
        o_ref[rows, :] = h * lax.rsqrt(jnp.mean(h * h, axis=-1, keepdims=True) + EPS) * fw_ref[...]


def _outproj(x2d, osb, proj, yssd, sb_norm_w, w_out, final_w, *, tm=512):
    rows = x2d.shape[0]
    return pl.pallas_call(
        functools.partial(_outproj_kernel, tm=tm, sub=256),
        grid=(rows // tm,),
        in_specs=[
            pl.BlockSpec((tm, D_MODEL), lambda i: (i, 0)),
            pl.BlockSpec((tm, SB_WIDTH), lambda i: (i, 0)),
            pl.BlockSpec((tm, SB_WIDTH), lambda i: (i, COL_GATE)),
            pl.BlockSpec((tm, SSD_WIDTH), lambda i: (i, 0)),
            pl.BlockSpec((1, SB_WIDTH), lambda i: (0, 0)),
            pl.BlockSpec((SB_WIDTH + SSD_WIDTH, D_MODEL), lambda i: (0, 0)),
            pl.BlockSpec((1, D_MODEL), lambda i: (0, 0)),
        ],
        out_specs=pl.BlockSpec((tm, D_MODEL), lambda i: (i, 0)),
        out_shape=jax.ShapeDtypeStruct((rows, D_MODEL), F32),
        compiler_params=pltpu.CompilerParams(dimension_semantics=("arbitrary",)),
        name="outproj",
    )(x2d, osb, proj, yssd, sb_norm_w, w_out, final_w)


def _layer(x2d, meta, norm_w, w_in, conv_w, conv_b, dt_bias, a_log, d_skip, sb_norm_w,
           ssd_norm_w, w_out, nbatch, seq):
    nh = SSD_HEADS
    w_main = w_in.astype(BF16)
    w_dt = jnp.pad(w_in[:, D_MAIN:], ((0, 0), (0, LANES - nh))).astype(BF16)
    dtb = jnp.pad(dt_bias, (0, LANES - nh)).reshape(1, LANES)
    alog = jnp.pad(a_log, (0, LANES - nh)).reshape(1, LANES)
    norm_w = norm_w.reshape(1, D_MODEL)
    dsk_rep = jnp.repeat(d_skip, HEAD_DIM).reshape(1, SSD_WIDTH)
    e_mat = (jnp.arange(LANES)[:, None] == (jnp.arange(SSD_WIDTH)[None, :] // HEAD_DIM)).astype(BF16)
    conv_b = conv_b.reshape(1, SSD_XBC)
    ssd_norm_w = ssd_norm_w.reshape(1, SSD_WIDTH)

    zeros_t = jnp.zeros((8, SSD_XBC), F32)
    proj_m, dt_m, tail_meta = _inproj(meta, norm_w, w_main, w_dt, dtb, zeros_t, conv_w, conv_b,
                                      tm=N_META, rows_per_seq=N_META, emit_tail=True)
    pad = CHUNK - N_META
    proj_mp = jnp.pad(proj_m, ((pad, 0), (0, 0)))
    dt_mp = jnp.pad(dt_m, ((pad, 0), (0, 0)))
    zeros_s = jnp.zeros((SSD_GROUPS, SSD_STATE, 512), F32)
    _, s_meta = _ssd(proj_mp, dt_mp, zeros_s, alog, dsk_rep, ssd_norm_w, e_mat,
                     nbatch=1, ts=CHUNK, emit_state=True)
    kmeta = jnp.pad(proj_m[:, COL_K * 1024:(COL_K + 1) * 1024], ((0, LANES - N_META), (0, 0)))
    vmeta = jnp.pad(proj_m[:, COL_V * 1024:(COL_V + 1) * 1024], ((0, LANES - N_META), (0, 0)))

    proj, dt = _inproj(x2d, norm_w, w_main, w_dt, dtb, tail_meta, conv_w, conv_b,
                       tm=512, rows_per_seq=seq, emit_tail=False)
    (yssd,) = _ssd(proj, dt, s_meta, alog, dsk_rep, ssd_norm_w, e_mat,
                   nbatch=nbatch, ts=TQ, emit_state=False)
    osb = _attn(proj, kmeta, vmeta, nbatch=nbatch, seq=seq)
    return osb, proj, yssd


def kernel(x, meta_tokens, norm_w, w_in, conv_w, conv_b, dt_bias, a_log, d_skip, sb_norm_w,
           ssd_norm_w, w_out, final_norm_w):
    nbatch, seq, _ = x.shape
    assert norm_w.shape[0] == 1, "single-layer block"
    x2d = x.reshape(nbatch * seq, D_MODEL)
    osb, proj, yssd = _layer(x2d, meta_tokens, norm_w[0], w_in[0], conv_w[0], conv_b[0], dt_bias[0],
                             a_log[0], d_skip[0], sb_norm_w[0], ssd_norm_w[0], w_out[0], nbatch, seq)
    out = _outproj(x2d, osb, proj, yssd, sb_norm_w[0].reshape(1, SB_WIDTH), w_out[0].astype(BF16),
                   final_norm_w.reshape(1, D_MODEL))
    return out.reshape(nbatch, seq, D_MODEL)
```

```python
import functools
import math

import jax
import jax.numpy as jnp
from jax import lax
from jax.experimental import pallas as pl
from jax.experimental.pallas import tpu as pltpu

F32 = jnp.float32
BF16 = jnp.bfloat16

D_MODEL = 1024
N_META = 16
HEAD_DIM = 64
SB_WIDTH = 1024
SSD_WIDTH = 1024
SSD_HEADS = 16
SSD_GROUPS = 2
SSD_STATE = 128
SSD_CONV = 4
SSD_XBC = SSD_WIDTH + 2 * SSD_GROUPS * SSD_STATE
D_MAIN = 4 * SB_WIDTH + SSD_WIDTH + SSD_XBC
EPS = 1e-5

LANES = 128
PAIR = 2 * HEAD_DIM
CHUNK = 64
TQ = 256
NEG_BIG = -1e30
Q_SCALE = 1.0 / math.sqrt(HEAD_DIM)
C_SKIP = 111.0

COL_Q, COL_K, COL_V, COL_GATE, COL_Z, COL_XS = 0, 1, 2, 3, 4, 5
COL_B, COL_C = 24, 25
GATE0 = COL_GATE * 1024
XBC0 = COL_XS * 1024


def _softplus(x):
    return jnp.maximum(x, 0.0) + jnp.log(1.0 + jnp.exp(-jnp.abs(x)))


def _silu(x):
    return x / (1.0 + jnp.exp(-x))


def _split3(x):
    hi = x.astype(BF16)
    r1 = x - hi.astype(F32)
    mid = r1.astype(BF16)
    lo = (r1 - mid.astype(F32)).astype(BF16)
    return hi, mid, lo


def _dot(a, b):
    return jnp.dot(a, b, preferred_element_type=F32)


def _dot_nt(a, b):
    return lax.dot_general(a, b, (((1,), (1,)), ((), ())), preferred_element_type=F32)


def _dot_tn(a, b):
    return lax.dot_general(a, b, (((0,), (0,)), ((), ())), preferred_element_type=F32)


def _inproj_kernel(x_ref, nw_ref, w_ref, wdt_ref, dtb_ref, tail0_ref, cw_ref, cb_ref, *rest,
                   tm, sub, tn, tiles_per_seq, emit_tail):
    if emit_tail:
        proj_ref, dt_ref, tail_ref, ext_scr = rest
    else:
        proj_ref, dt_ref, ext_scr = rest

    @pl.when(pl.program_id(0) % tiles_per_seq == 0)
    def _():
        ext_scr[0:8, :] = tail0_ref[...]

    for r0 in range(0, tm, sub):
        rows = slice(r0, r0 + sub)
        x = x_ref[rows, :]
        u = x * lax.rsqrt(jnp.mean(x * x, axis=-1, keepdims=True) + EPS) * nw_ref[...]
        ub = u.astype(BF16)
        dt_ref[rows, :] = _softplus(_dot(ub, wdt_ref[...]) + dtb_ref[...])
        for c0 in range(XBC0, D_MAIN, tn):
            cc = slice(c0 - XBC0, c0 - XBC0 + tn)
            ext_scr[8:8 + sub, cc] = _dot(ub, w_ref[:, c0:c0 + tn])
            conv = cb_ref[:, cc] + cw_ref[0:1, cc] * ext_scr[5:5 + sub, cc]
            for k in range(1, SSD_CONV):
                conv = conv + cw_ref[k:k + 1, cc] * ext_scr[5 + k:5 + k + sub, cc]
            proj_ref[rows, c0:c0 + tn] = _silu(conv).astype(BF16)
            ext_scr[0:8, cc] = ext_scr[sub:sub + 8, cc]
        for c0 in list(range(GATE0, XBC0, tn)) + list(range(0, GATE0, tn)):
            acc = _dot(ub, w_ref[:, c0:c0 + tn])
            if c0 < SB_WIDTH:
                acc = acc * Q_SCALE
            elif c0 >= GATE0:
                acc = _silu(acc)
            proj_ref[rows, c0:c0 + tn] = acc.astype(BF16)
    if emit_tail:
        tail_ref[...] = ext_scr[0:8, :]


def _inproj(x2d, norm_w, w_all, w_dt, dt_bias, tail0, conv_w, conv_b, *, tm, rows_per_seq, emit_tail, tn=512):
    rows = x2d.shape[0]
    sub = min(tm, 256)
    assert SB_WIDTH % tn == 0 and GATE0 % tn == 0 and XBC0 % tn == 0 and D_MAIN % tn == 0
    assert rows_per_seq % tm == 0 and tm % sub == 0
    const = lambda i: (0, 0)
    out_specs = [pl.BlockSpec((tm, D_MAIN), lambda i: (i, 0)), pl.BlockSpec((tm, LANES), lambda i: (i, 0))]
    out_shape = [jax.ShapeDtypeStruct((rows, D_MAIN), BF16), jax.ShapeDtypeStruct((rows, LANES), F32)]
    if emit_tail:
        out_specs.append(pl.BlockSpec((8, SSD_XBC), const))
        out_shape.append(jax.ShapeDtypeStruct((8, SSD_XBC), F32))
    return pl.pallas_call(
        functools.partial(_inproj_kernel, tm=tm, sub=sub, tn=tn, tiles_per_seq=rows_per_seq // tm,
                          emit_tail=emit_tail),
        grid=(rows // tm,),
        in_specs=[
            pl.BlockSpec((tm, D_MODEL), lambda i: (i, 0)),
            pl.BlockSpec((1, D_MODEL), const),
            pl.BlockSpec((D_MODEL, D_MAIN), const, pipeline_mode=pl.Buffered(1)),
            pl.BlockSpec((D_MODEL, LANES), const),
            pl.BlockSpec((1, LANES), const),
            pl.BlockSpec((8, SSD_XBC), const),
            pl.BlockSpec((SSD_CONV, SSD_XBC), const),
            pl.BlockSpec((1, SSD_XBC), const),
        ],
        out_specs=out_specs,
        out_shape=out_shape,
        scratch_shapes=[pltpu.VMEM((sub + 8, SSD_XBC), F32)],
        compiler_params=pltpu.CompilerParams(dimension_semantics=("arbitrary",)),
        name="inproj",
    )(x2d, norm_w, w_all, w_dt, dt_bias, tail0, conv_w, conv_b)


def _ssd_kernel(zact_ref, xs_ref, b_ref, c_ref, dt_ref, s0_ref, alog_ref, dsk_ref, nw_ref, e_ref,
                *rest, ts, emit_state):
    if emit_state:
        y_ref, sfin_ref, s_scr, y_scr = rest
    else:
        y_ref, s_scr, y_scr = rest

    @pl.when(pl.program_id(1) == 0)
    def _():
        s_scr[...] = s0_ref[...]

    nchunk = ts // CHUNK
    a_row = -jnp.exp(alog_ref[...])
    li = lax.broadcasted_iota(jnp.int32, (ts, ts), 0)
    si = lax.broadcasted_iota(jnp.int32, (ts, ts), 1)
    t_incl = jnp.logical_and(si <= li, si // CHUNK == li // CHUNK).astype(BF16)
    l2 = lax.broadcasted_iota(jnp.int32, (CHUNK, LANES), 0)
    lane2 = lax.broadcasted_iota(jnp.int32, (CHUNK, LANES), 1)
    mask2 = (lane2 % CHUNK) <= l2
    low_half = lane2 < HEAD_DIM
    e_mat = e_ref[...]

    xs = xs_ref[...].astype(F32)
    dt = dt_ref[...]
    h3 = _split3(dt * a_row)
    acum = _dot(t_incl, h3[0]) + _dot(t_incl, h3[1]) + _dot(t_incl, h3[2])
    st3 = _split3(jnp.concatenate([dt, acum], axis=0))
    e2 = jnp.concatenate([e_mat, e_mat], axis=0)
    rep = _dot(jnp.concatenate([st3[0], st3[1]], axis=1), e2)
    dtr = rep[0:ts]
    acr = rep[ts:2 * ts] + _dot(st3[2][ts:2 * ts], e_mat)
    xdt = xs * dtr
    ebase = jnp.exp(acr)

    chunks = [slice(c * CHUNK, (c + 1) * CHUNK) for c in range(nchunk)]

    yoffs = []
    for c, r in enumerate(chunks):
        atot = acr[(c + 1) * CHUNK - 1:(c + 1) * CHUNK, :]
        xw = (xdt[r] * jnp.exp(atot - acr[r])).astype(BF16)
        etot = jnp.exp(atot)
        for g in range(SSD_GROUPS):
            gs = slice(g * 512, (g + 1) * 512)
            ns = slice(g * SSD_STATE, (g + 1) * SSD_STATE)
            sg = s_scr[g]
            yoffs.append(_dot(c_ref[r, ns], sg.astype(BF16)) * ebase[r, gs])
            s_scr[g] = sg * etot[:, gs] + _dot_tn(b_ref[r, ns], xw[:, gs])

    wts = [jnp.concatenate([acum[r], pltpu.roll(acum[r], LANES - 1, axis=1)], axis=0).T for r in chunks]
    cb2s = []
    for r in chunks:
        for g in range(SSD_GROUPS):
            ns = slice(g * SSD_STATE, (g + 1) * SSD_STATE)
            bg = b_ref[r, ns]
            cb2s.append(_dot_nt(c_ref[r, ns], jnp.concatenate([bg, bg], axis=0)))
    npairs = SSD_HEADS // 2
    m2s, xblks = [], []
    for c, r in enumerate(chunks):
        for p in range(npairs):
            cols = slice(p * PAIR, (p + 1) * PAIR)
            seg = acr[r, cols] - wts[c][2 * p:2 * p + 1, :]
            m2s.append((cb2s[c * SSD_GROUPS + p // 4] * jnp.exp(jnp.where(mask2, seg, NEG_BIG))).astype(BF16))
            x2 = xdt[r, cols].astype(BF16)
            zero = jnp.zeros_like(x2)
            xblks.append(jnp.concatenate([jnp.where(low_half, x2, zero),
                                          jnp.where(low_half, zero, x2)], axis=0))
    ydiags = [_dot(m2, xblk) for m2, xblk in zip(m2s, xblks)]
    for c, r in enumerate(chunks):
        for p in range(npairs):
            cols = slice(p * PAIR, (p + 1) * PAIR)
            yoff = yoffs[c * SSD_GROUPS + p // 4]
            y_scr[r, cols] = (ydiags[c * npairs + p] + yoff[:, (p % 4) * PAIR:(p % 4 + 1) * PAIR]
                              + xs[r, cols] * dsk_ref[:, cols])

    yg = y_scr[...] * zact_ref[...].astype(F32)
    y_ref[...] = (yg * lax.rsqrt(jnp.mean(yg * yg, axis=-1, keepdims=True) + EPS)
                  * nw_ref[...]).astype(BF16)
    if emit_state:
        sfin_ref[...] = s_scr[...]


def _ssd(proj, dt, s0, a_log, dsk_rep, norm_w, e_mat, *, nbatch, ts, emit_state):
    rows = proj.shape[0]
    nsteps = rows // (nbatch * ts)
    rb = lambda b, s: b * nsteps + s
    const = lambda b, s: (0, 0)
    in_specs = [
        pl.BlockSpec((ts, 1024), lambda b, s: (rb(b, s), COL_Z)),
        pl.BlockSpec((ts, 1024), lambda b, s: (rb(b, s), COL_XS)),
        pl.BlockSpec((ts, 256), lambda b, s: (rb(b, s), COL_B)),
        pl.BlockSpec((ts, 256), lambda b, s: (rb(b, s), COL_C)),
        pl.BlockSpec((ts, LANES), lambda b, s: (rb(b, s), 0)),
        pl.BlockSpec((SSD_GROUPS, SSD_STATE, 512), lambda b, s: (0, 0, 0)),
        pl.BlockSpec((1, LANES), const),
        pl.BlockSpec((1, SSD_WIDTH), const),
        pl.BlockSpec((1, SSD_WIDTH), const),
        pl.BlockSpec((LANES, SSD_WIDTH), const),
    ]
    out_specs = [pl.BlockSpec((ts, SSD_WIDTH), lambda b, s: (rb(b, s), 0))]
    out_shape = [jax.ShapeDtypeStruct((rows, SSD_WIDTH), BF16)]
    if emit_state:
        out_specs.append(pl.BlockSpec((SSD_GROUPS, SSD_STATE, 512), lambda b, s: (0, 0, 0)))
        out_shape.append(jax.ShapeDtypeStruct((SSD_GROUPS, SSD_STATE, 512), F32))
    return pl.pallas_call(
        functools.partial(_ssd_kernel, ts=ts, emit_state=emit_state),
        grid=(nbatch, nsteps),
        in_specs=in_specs,
        out_specs=out_specs,
        out_shape=out_shape,
        scratch_shapes=[
            pltpu.VMEM((SSD_GROUPS, SSD_STATE, 512), F32),
            pltpu.VMEM((ts, SSD_WIDTH), F32),
        ],
        compiler_params=pltpu.CompilerParams(dimension_semantics=("arbitrary", "arbitrary")),
        name="ssd_meta" if emit_state else "ssd",
    )(proj, proj, proj, proj, dt, s0, a_log, dsk_rep, norm_w, e_mat)


def _sb_tiles(items, tmat):
    n = len(items)
    ys = [_dot_nt(it[0], it[1]).astype(BF16) for it in items]
    sps, owns = [], []
    for y, it in zip(ys, items):
        lg = jnp.log(1.0 + jnp.exp(-jnp.abs(y)))
        sp = jnp.maximum(y, 0.0) + lg
        owns.append(jnp.minimum(y, 0.0) - lg)
        if it[3] is not None:
            sp = jnp.where(it[3], sp, jnp.zeros_like(sp))
        sps.append(sp)
    excls = [_dot(sp, tmat[0:sp.shape[1], 0:sp.shape[1]]) for sp in sps]
    rss = [excls[i][:, 0:1] + sps[i][:, 0:1].astype(F32) for i in range(n)]
    c_in, c_out = [], []
    for i, it in enumerate(items):
        c = c_out[it[6]] if it[6] is not None else it[4]
        c_in.append(c)
        c_out.append(rss[i] if c is None else c + rss[i])
    ws = []
    for i, it in enumerate(items):
        t = excls[i] if c_in[i] is None else excls[i] + c_in[i]
        w = jnp.exp(owns[i] - t.astype(BF16))
        if it[3] is not None:
            w = jnp.where(it[3], w, jnp.zeros_like(w))
        ws.append(w)
    pvs = [_dot(ws[i], items[i][2]) for i in range(n)]
    a_out = []
    for i, it in enumerate(items):
        a = a_out[it[6]] if it[6] is not None else it[5]
        a_out.append(pvs[i] if a is None else a + pvs[i])
    return c_out, a_out


def _attn_kernel(q_ref, k_ref, v_ref, km_ref, vm_ref, o_ref, c_scr, acc_scr, *, nq, npair):
    nchain = 2 * npair
    lane = lax.broadcasted_iota(jnp.int32, (TQ, PAIR), 1)
    low = lane < HEAD_DIM
    row = lax.broadcasted_iota(jnp.int32, (TQ, TQ), 0)
    col = lax.broadcasted_iota(jnp.int32, (TQ, TQ), 1)
    tmat = (row > col).astype(BF16)
    diag_mask = col < row
    meta_mask = lax.broadcasted_iota(jnp.int32, (TQ, LANES), 1) < N_META

    def cols(p):
        return slice(p * PAIR, (p + 1) * PAIR)

    def load_q(r0):
        qs = []
        for p in range(npair):
            q2 = q_ref[pl.ds(r0, TQ), cols(p)]
            zero = jnp.zeros_like(q2)
            qs += [jnp.where(low, q2, zero), jnp.where(low, zero, q2)]
        return qs

    def kv_tile(qs, k0, mask, state):
        items = []
        for i in range(nchain):
            kb = k_ref[pl.ds(k0, TQ), cols(i // 2)]
            vb = v_ref[pl.ds(k0, TQ), cols(i // 2)]
            if state == "scratch":
                items.append((qs[i], kb, vb, mask, c_scr[i], acc_scr[i], None))
            elif state == "fresh":
                items.append((qs[i], kb, vb, mask, None, None, None))
            else:
                items.append((qs[i], kb, vb, mask, None, None, state + i))
        return items

    def commit(cs, accs):
        for i in range(nchain):
            c_scr[i] = cs[i]
            acc_scr[i] = accs[i]

    def unfinished():
        m = c_scr[0]
        for i in range(1, nchain):
            m = jnp.minimum(m, c_scr[i])
        return (jnp.min(m) < C_SKIP).astype(jnp.int32)

    def meta_tile(qs):
        items = [(qs[i], km_ref[:, cols(i // 2)], vm_ref[:, cols(i // 2)], meta_mask,
                  c_scr[i], acc_scr[i], None) for i in range(nchain)]
        commit(*_sb_tiles(items, tmat))

    def write_out(r0):
        for p in range(npair):
            o_ref[pl.ds(r0, TQ), cols(p)] = jnp.where(low, acc_scr[2 * p], acc_scr[2 * p + 1]).astype(BF16)

    qs0 = load_q(0)
    commit(*_sb_tiles(kv_tile(qs0, 0, diag_mask, "fresh"), tmat))
    meta_tile(qs0)
    write_out(0)

    def qblock(qi, carry):
        r0 = pl.multiple_of(qi * TQ, TQ)
        qs = load_q(r0)
        items = (kv_tile(qs, r0, diag_mask, "fresh")
                 + kv_tile(qs, pl.multiple_of(r0 - TQ, TQ), None, 0))
        cs, accs = _sb_tiles(items, tmat)
        commit(cs[nchain:], accs[nchain:])

        def cond(st):
            return jnp.logical_and(st[0] < qi, st[1] > 0)

        def body(st):
            k0 = pl.multiple_of((qi - 1 - st[0]) * TQ, TQ)
            commit(*_sb_tiles(kv_tile(qs, k0, None, "scratch"), tmat))
            return st[0] + 1, unfinished()

        _, more = lax.while_loop(cond, body, (jnp.int32(1), unfinished()))

        @pl.when(more > 0)
        def _():
            meta_tile(qs)

        write_out(r0)
        return carry

    lax.fori_loop(1, nq, qblock, 0)


def _attn(proj, kmeta, vmeta, *, nbatch, seq, npair=4):
    width = npair * PAIR
    nblk = SB_WIDTH // width
    return pl.pallas_call(
        functools.partial(_attn_kernel, nq=seq // TQ, npair=npair),
        grid=(nbatch, nblk),
        in_specs=[
            pl.BlockSpec((seq, width), lambda b, p: (b, COL_Q * nblk + p)),
            pl.BlockSpec((seq, width), lambda b, p: (b, COL_K * nblk + p)),
            pl.BlockSpec((seq, width), lambda b, p: (b, COL_V * nblk + p)),
            pl.BlockSpec((LANES, width), lambda b, p: (0, p)),
            pl.BlockSpec((LANES, width), lambda b, p: (0, p)),
        ],
        out_specs=pl.BlockSpec((seq, width), lambda b, p: (b, p)),
        out_shape=jax.ShapeDtypeStruct((nbatch * seq, SB_WIDTH), BF16),
        scratch_shapes=[pltpu.VMEM((2 * npair, TQ, 1), F32), pltpu.VMEM((2 * npair, TQ, PAIR), F32)],
        compiler_params=pltpu.CompilerParams(dimension_semantics=("arbitrary", "arbitrary")),
        name="sb_attn",
    )(proj, proj, proj, kmeta, vmeta)


def _outproj_kernel(x_ref, osb_ref, gate_ref, yssd_ref, sbw_ref, wo_ref, fw_ref, o_ref, *, tm, sub):
    nsub = tm // sub

    def gated_norm(s):
        rows = slice(s * sub, (s + 1) * sub)
        ys = osb_ref[rows, :].astype(F32) * gate_ref[rows, :].astype(F32)
        return (ys * lax.rsqrt(jnp.mean(ys * ys, axis=-1, keepdims=True) + EPS) * sbw_ref[...]).astype(BF16)

    ysb = gated_norm(0)
    for s in range(nsub):
        rows = slice(s * sub, (s + 1) * sub)
        acc = (_dot(ysb, wo_ref[0:SB_WIDTH, :])
               + _dot(yssd_ref[rows, :], wo_ref[SB_WIDTH:SB_WIDTH + SSD_WIDTH, :]))
        if s + 1 < nsub:
            ysb = gated_norm(s + 1)
        h = x_ref[rows, :] + acc
        o_ref[rows, :] = h * lax.rsqrt(jnp.mean(h * h, axis=-1, keepdims=True) + EPS) * fw_ref[...]


def _outproj(x2d, osb, proj, yssd, sb_norm_w, w_out, final_w, *, tm=1024):
    rows = x2d.shape[0]
    return pl.pallas_call(
        functools.partial(_outproj_kernel, tm=tm, sub=256),
        grid=(rows // tm,),
        in_specs=[
            pl.BlockSpec((tm, D_MODEL), lambda i: (i, 0)),
            pl.BlockSpec((tm, SB_WIDTH), lambda i: (i, 0)),
            pl.BlockSpec((tm, SB_WIDTH), lambda i: (i, COL_GATE)),
            pl.BlockSpec((tm, SSD_WIDTH), lambda i: (i, 0)),
            pl.BlockSpec((1, SB_WIDTH), lambda i: (0, 0)),
            pl.BlockSpec((SB_WIDTH + SSD_WIDTH, D_MODEL), lambda i: (0, 0)),
            pl.BlockSpec((1, D_MODEL), lambda i: (0, 0)),
        ],
        out_specs=pl.BlockSpec((tm, D_MODEL), lambda i: (i, 0)),
        out_shape=jax.ShapeDtypeStruct((rows, D_MODEL), F32),
        compiler_params=pltpu.CompilerParams(dimension_semantics=("arbitrary",)),
        name="outproj",
    )(x2d, osb, proj, yssd, sb_norm_w, w_out, final_w)


def _layer(x2d, meta, norm_w, w_in, conv_w, conv_b, dt_bias, a_log, d_skip, sb_norm_w,
           ssd_norm_w, w_out, nbatch, seq):
    nh = SSD_HEADS
    w_main = w_in.astype(BF16)
    w_dt = jnp.pad(w_in[:, D_MAIN:], ((0, 0), (0, LANES - nh))).astype(BF16)
    dtb = jnp.pad(dt_bias, (0, LANES - nh)).reshape(1, LANES)
    alog = jnp.pad(a_log, (0, LANES - nh)).reshape(1, LANES)
    norm_w = norm_w.reshape(1, D_MODEL)
    dsk_rep = jnp.repeat(d_skip, HEAD_DIM).reshape(1, SSD_WIDTH)
    e_mat = (jnp.arange(LANES)[:, None] == (jnp.arange(SSD_WIDTH)[None, :] // HEAD_DIM)).astype(BF16)
    conv_b = conv_b.reshape(1, SSD_XBC)
    ssd_norm_w = ssd_norm_w.reshape(1, SSD_WIDTH)

    zeros_t = jnp.zeros((8, SSD_XBC), F32)
    proj_m, dt_m, tail_meta = _inproj(meta, norm_w, w_main, w_dt, dtb, zeros_t, conv_w, conv_b,
                                      tm=N_META, rows_per_seq=N_META, emit_tail=True)
    pad = CHUNK - N_META
    proj_mp = jnp.pad(proj_m, ((pad, 0), (0, 0)))
    dt_mp = jnp.pad(dt_m, ((pad, 0), (0, 0)))
    zeros_s = jnp.zeros((SSD_GROUPS, SSD_STATE, 512), F32)
    _, s_meta = _ssd(proj_mp, dt_mp, zeros_s, alog, dsk_rep, ssd_norm_w, e_mat,
                     nbatch=1, ts=CHUNK, emit_state=True)
    kmeta = jnp.pad(proj_m[:, COL_K * 1024:(COL_K + 1) * 1024], ((0, LANES - N_META), (0, 0)))
    vmeta = jnp.pad(proj_m[:, COL_V * 1024:(COL_V + 1) * 1024], ((0, LANES - N_META), (0, 0)))

    proj, dt = _inproj(x2d, norm_w, w_main, w_dt, dtb, tail_meta, conv_w, conv_b,
                       tm=512, rows_per_seq=seq, emit_tail=False)
    (yssd,) = _ssd(proj, dt, s_meta, alog, dsk_rep, ssd_norm_w, e_mat,
                   nbatch=nbatch, ts=TQ, emit_state=False)
    osb = _attn(proj, kmeta, vmeta, nbatch=nbatch, seq=seq)
    return osb, proj, yssd


def kernel(x, meta_tokens, norm_w, w_in, conv_w, conv_b, dt_bias, a_log, d_skip, sb_norm_w,
           ssd_norm_w, w_out, final_norm_w):
    nbatch, seq, _ = x.shape
    assert norm_w.shape[0] == 1, "single-layer block"
    x2d = x.reshape(nbatch * seq, D_MODEL)
    osb, proj, yssd = _layer(x2d, meta_tokens, norm_w[0], w_in[0], conv_w[0], conv_b[0], dt_bias[0],
                             a_log[0], d_skip[0], sb_norm_w[0], ssd_norm_w[0], w_out[0], nbatch, seq)
    out = _outproj(x2d, osb, proj, yssd, sb_norm_w[0].reshape(1, SB_WIDTH), w_out[0].astype(BF16),
                   final_norm_w.reshape(1, D_MODEL))
    return out.reshape(nbatch, seq, D_MODEL)
```

```python
import functools
import math

import jax
import jax.numpy as jnp
from jax import lax
from jax.experimental import pallas as pl
from jax.experimental.pallas import tpu as pltpu

F32 = jnp.float32
BF16 = jnp.bfloat16

D_MODEL = 1024
N_META = 16
HEAD_DIM = 64
SB_WIDTH = 1024
SSD_WIDTH = 1024
SSD_HEADS = 16
SSD_GROUPS = 2
SSD_STATE = 128
SSD_CONV = 4
SSD_XBC = SSD_WIDTH + 2 * SSD_GROUPS * SSD_STATE
D_MAIN = 4 * SB_WIDTH + SSD_WIDTH + SSD_XBC
EPS = 1e-5

LANES = 128
PAIR = 2 * HEAD_DIM
CHUNK = 64
TQ = 256
NEG_BIG = -1e30
Q_SCALE = 1.0 / math.sqrt(HEAD_DIM)
C_SKIP = 111.0

COL_Q, COL_K, COL_V, COL_GATE, COL_Z, COL_XS = 0, 1, 2, 3, 4, 5
COL_B, COL_C = 24, 25
GATE0 = COL_GATE * 1024
XBC0 = COL_XS * 1024


def _softplus(x):
    return jnp.maximum(x, 0.0) + jnp.log(1.0 + jnp.exp(-jnp.abs(x)))


def _silu(x):
    h = 0.5 * x
    return h + h * jnp.tanh(h)


def _split3(x):
    hi = x.astype(BF16)
    r1 = x - hi.astype(F32)
    mid = r1.astype(BF16)
    lo = (r1 - mid.astype(F32)).astype(BF16)
    return hi, mid, lo


def _dot(a, b):
    return jnp.dot(a, b, preferred_element_type=F32)


def _dot_nt(a, b):
    return lax.dot_general(a, b, (((1,), (1,)), ((), ())), preferred_element_type=F32)


def _dot_tn(a, b):
    return lax.dot_general(a, b, (((0,), (0,)), ((), ())), preferred_element_type=F32)


def _inproj_kernel(x_ref, nw_ref, w_ref, wdt_ref, dtb_ref, tail0_ref, cw_ref, cb_ref, *rest,
                   tm, sub, tn, tiles_per_seq, emit_tail):
    if emit_tail:
        proj_ref, dt_ref, tail_ref, ext_scr = rest
    else:
        proj_ref, dt_ref, ext_scr = rest

    @pl.when(pl.program_id(0) % tiles_per_seq == 0)
    def _():
        ext_scr[0:8, :] = tail0_ref[...]

    for r0 in range(0, tm, sub):
        rows = slice(r0, r0 + sub)
        x = x_ref[rows, :]
        u = x * lax.rsqrt(jnp.mean(x * x, axis=-1, keepdims=True) + EPS) * nw_ref[...]
        ub = u.astype(BF16)
        dt_ref[rows, :] = _softplus(_dot(ub, wdt_ref[...]) + dtb_ref[...])
        for c0 in range(XBC0, D_MAIN, tn):
            cc = slice(c0 - XBC0, c0 - XBC0 + tn)
            ext_scr[8:8 + sub, cc] = _dot(ub, w_ref[:, c0:c0 + tn])
            ea = ext_scr[:, cc]
            e1 = pltpu.roll(ea, 1, axis=0)
            pa = cw_ref[3:4, cc] * ea + cw_ref[2:3, cc] * e1
            pb = cw_ref[1:2, cc] * ea + cw_ref[0:1, cc] * e1
            conv = cb_ref[:, cc] + pa[8:8 + sub] + pb[6:6 + sub]
            proj_ref[rows, c0:c0 + tn] = _silu(conv).astype(BF16)
            ext_scr[0:8, cc] = ext_scr[sub:sub + 8, cc]
        for c0 in list(range(GATE0, XBC0, tn)) + list(range(0, GATE0, tn)):
            acc = _dot(ub, w_ref[:, c0:c0 + tn])
            if c0 < SB_WIDTH:
                acc = acc * Q_SCALE
            elif c0 >= GATE0:
                acc = _silu(acc)
            proj_ref[rows, c0:c0 + tn] = acc.astype(BF16)
    if emit_tail:
        tail_ref[...] = ext_scr[0:8, :]


def _inproj(x2d, norm_w, w_all, w_dt, dt_bias, tail0, conv_w, conv_b, *, tm, rows_per_seq, emit_tail, tn=512):
    rows = x2d.shape[0]
    sub = min(tm, 256)
    assert SB_WIDTH % tn == 0 and GATE0 % tn == 0 and XBC0 % tn == 0 and D_MAIN % tn == 0
    assert rows_per_seq % tm == 0 and tm % sub == 0
    const = lambda i: (0, 0)
    out_specs = [pl.BlockSpec((tm, D_MAIN), lambda i: (i, 0)), pl.BlockSpec((tm, LANES), lambda i: (i, 0))]
    out_shape = [jax.ShapeDtypeStruct((rows, D_MAIN), BF16), jax.ShapeDtypeStruct((rows, LANES), F32)]
    if emit_tail:
        out_specs.append(pl.BlockSpec((8, SSD_XBC), const))
        out_shape.append(jax.ShapeDtypeStruct((8, SSD_XBC), F32))
    return pl.pallas_call(
        functools.partial(_inproj_kernel, tm=tm, sub=sub, tn=tn, tiles_per_seq=rows_per_seq // tm,
                          emit_tail=emit_tail),
        grid=(rows // tm,),
        in_specs=[
            pl.BlockSpec((tm, D_MODEL), lambda i: (i, 0)),
            pl.BlockSpec((1, D_MODEL), const),
            pl.BlockSpec((D_MODEL, D_MAIN), const, pipeline_mode=pl.Buffered(1)),
            pl.BlockSpec((D_MODEL, LANES), const),
            pl.BlockSpec((1, LANES), const),
            pl.BlockSpec((8, SSD_XBC), const),
            pl.BlockSpec((SSD_CONV, SSD_XBC), const),
            pl.BlockSpec((1, SSD_XBC), const),
        ],
        out_specs=out_specs,
        out_shape=out_shape,
        scratch_shapes=[pltpu.VMEM((sub + 8, SSD_XBC), F32)],
        compiler_params=pltpu.CompilerParams(dimension_semantics=("arbitrary",)),
        name="inproj",
    )(x2d, norm_w, w_all, w_dt, dt_bias, tail0, conv_w, conv_b)


def _ssd_kernel(zact_ref, xs_ref, b_ref, c_ref, dt_ref, s0_ref, alog_ref, dsk_ref, nw_ref, e_ref,
                *rest, ts, emit_state):
    if emit_state:
        y_ref, sfin_ref, s_scr, y_scr = rest
    else:
        y_ref, s_scr, y_scr = rest

    @pl.when(pl.program_id(1) == 0)
    def _():
        s_scr[...] = s0_ref[...]

    nchunk = ts // CHUNK
    a_row = -jnp.exp(alog_ref[...])
    li = lax.broadcasted_iota(jnp.int32, (ts, ts), 0)
    si = lax.broadcasted_iota(jnp.int32, (ts, ts), 1)
    t_incl = jnp.logical_and(si <= li, si // CHUNK == li // CHUNK).astype(BF16)
    l2 = lax.broadcasted_iota(jnp.int32, (CHUNK, LANES), 0)
    lane2 = lax.broadcasted_iota(jnp.int32, (CHUNK, LANES), 1)
    mask2 = (lane2 % CHUNK) <= l2
    low_half = lane2 < HEAD_DIM
    e_mat = e_ref[...]

    xs = xs_ref[...].astype(F32)
    dt = dt_ref[...]
    h3 = _split3(dt * a_row)
    acum = _dot(t_incl, h3[0]) + _dot(t_incl, h3[1]) + _dot(t_incl, h3[2])
    st3 = _split3(jnp.concatenate([dt, acum], axis=0))
    e2 = jnp.concatenate([e_mat, e_mat], axis=0)
    rep = _dot(jnp.concatenate([st3[0], st3[1]], axis=1), e2)
    dtr = rep[0:ts]
    acr = rep[ts:2 * ts] + _dot(st3[2][ts:2 * ts], e_mat)
    xdt = xs * dtr
    ebase = jnp.exp(acr)

    chunks = [slice(c * CHUNK, (c + 1) * CHUNK) for c in range(nchunk)]

    yoffs = []
    for c, r in enumerate(chunks):
        atot = acr[(c + 1) * CHUNK - 1:(c + 1) * CHUNK, :]
        xw = (xdt[r] * jnp.exp(atot - acr[r])).astype(BF16)
        etot = jnp.exp(atot)
        for g in range(SSD_GROUPS):
            gs = slice(g * 512, (g + 1) * 512)
            ns = slice(g * SSD_STATE, (g + 1) * SSD_STATE)
            sg = s_scr[g]
            yoffs.append(_dot(c_ref[r, ns], sg.astype(BF16)) * ebase[r, gs])
            s_scr[g] = sg * etot[:, gs] + _dot_tn(b_ref[r, ns], xw[:, gs])

    wts = [jnp.concatenate([acum[r], pltpu.roll(acum[r], LANES - 1, axis=1)], axis=0).T for r in chunks]
    cb2s = []
    for r in chunks:
        for g in range(SSD_GROUPS):
            ns = slice(g * SSD_STATE, (g + 1) * SSD_STATE)
            bg = b_ref[r, ns]
            cb2s.append(_dot_nt(c_ref[r, ns], jnp.concatenate([bg, bg], axis=0)))
    npairs = SSD_HEADS // 2
    m2s, xblks = [], []
    for c, r in enumerate(chunks):
        for p in range(npairs):
            cols = slice(p * PAIR, (p + 1) * PAIR)
            seg = acr[r, cols] - wts[c][2 * p:2 * p + 1, :]
            m2s.append((cb2s[c * SSD_GROUPS + p // 4] * jnp.exp(jnp.where(mask2, seg, NEG_BIG))).astype(BF16))
            x2 = xdt[r, cols].astype(BF16)
            zero = jnp.zeros_like(x2)
            xblks.append(jnp.concatenate([jnp.where(low_half, x2, zero),
                                          jnp.where(low_half, zero, x2)], axis=0))
    ydiags = [_dot(m2, xblk) for m2, xblk in zip(m2s, xblks)]
    for c, r in enumerate(chunks):
        for p in range(npairs):
            cols = slice(p * PAIR, (p + 1) * PAIR)
            yoff = yoffs[c * SSD_GROUPS + p // 4]
            y_scr[r, cols] = (ydiags[c * npairs + p] + yoff[:, (p % 4) * PAIR:(p % 4 + 1) * PAIR]
                              + xs[r, cols] * dsk_ref[:, cols])

    yg = y_scr[...] * zact_ref[...].astype(F32)
    y_ref[...] = (yg * lax.rsqrt(jnp.mean(yg * yg, axis=-1, keepdims=True) + EPS)
                  * nw_ref[...]).astype(BF16)
    if emit_state:
        sfin_ref[...] = s_scr[...]


def _ssd(proj, dt, s0, a_log, dsk_rep, norm_w, e_mat, *, nbatch, ts, emit_state):
    rows = proj.shape[0]
    nsteps = rows // (nbatch * ts)
    rb = lambda b, s: b * nsteps + s
    const = lambda b, s: (0, 0)
    in_specs = [
        pl.BlockSpec((ts, 1024), lambda b, s: (rb(b, s), COL_Z)),
        pl.BlockSpec((ts, 1024), lambda b, s: (rb(b, s), COL_XS)),
        pl.BlockSpec((ts, 256), lambda b, s: (rb(b, s), COL_B)),
        pl.BlockSpec((ts, 256), lambda b, s: (rb(b, s), COL_C)),
        pl.BlockSpec((ts, LANES), lambda b, s: (rb(b, s), 0)),
        pl.BlockSpec((SSD_GROUPS, SSD_STATE, 512), lambda b, s: (0, 0, 0)),
        pl.BlockSpec((1, LANES), const),
        pl.BlockSpec((1, SSD_WIDTH), const),
        pl.BlockSpec((1, SSD_WIDTH), const),
        pl.BlockSpec((LANES, SSD_WIDTH), const),
    ]
    out_specs = [pl.BlockSpec((ts, SSD_WIDTH), lambda b, s: (rb(b, s), 0))]
    out_shape = [jax.ShapeDtypeStruct((rows, SSD_WIDTH), BF16)]
    if emit_state:
        out_specs.append(pl.BlockSpec((SSD_GROUPS, SSD_STATE, 512), lambda b, s: (0, 0, 0)))
        out_shape.append(jax.ShapeDtypeStruct((SSD_GROUPS, SSD_STATE, 512), F32))
    return pl.pallas_call(
        functools.partial(_ssd_kernel, ts=ts, emit_state=emit_state),
        grid=(nbatch, nsteps),
        in_specs=in_specs,
        out_specs=out_specs,
        out_shape=out_shape,
        scratch_shapes=[
            pltpu.VMEM((SSD_GROUPS, SSD_STATE, 512), F32),
            pltpu.VMEM((ts, SSD_WIDTH), F32),
        ],
        compiler_params=pltpu.CompilerParams(dimension_semantics=("arbitrary", "arbitrary")),
        name="ssd_meta" if emit_state else "ssd",
    )(proj, proj, proj, proj, dt, s0, a_log, dsk_rep, norm_w, e_mat)


def _sb_tiles(items, tmat):
    n = len(items)
    ys = [_dot_nt(it[0], it[1]).astype(BF16) for it in items]
    sps, owns = [], []
    for y, it in zip(ys, items):
        lg = jnp.log(1.0 + jnp.exp(-jnp.abs(y)))
        sp = jnp.maximum(y, 0.0) + lg
        owns.append(jnp.minimum(y, 0.0) - lg)
        if it[3] is not None:
            sp = jnp.where(it[3], sp, jnp.zeros_like(sp))
        sps.append(sp)
    excls = [_dot(sp, tmat[0:sp.shape[1], 0:sp.shape[1]]) for sp in sps]
    rss = [excls[i][:, 0:1] + sps[i][:, 0:1].astype(F32) for i in range(n)]
    c_in, c_out = [], []
    for i, it in enumerate(items):
        c = c_out[it[6]] if it[6] is not None else it[4]
        c_in.append(c)
        c_out.append(rss[i] if c is None else c + rss[i])
    ws = []
    for i, it in enumerate(items):
        t = excls[i] if c_in[i] is None else excls[i] + c_in[i]
        w = jnp.exp(owns[i] - t.astype(BF16))
        if it[3] is not None:
            w = jnp.where(it[3], w, jnp.zeros_like(w))
        ws.append(w)
    pvs = [_dot(ws[i], items[i][2]) for i in range(n)]
    a_out = []
    for i, it in enumerate(items):
        a = a_out[it[6]] if it[6] is not None else it[5]
        a_out.append(pvs[i] if a is None else a + pvs[i])
    return c_out, a_out


def _attn_kernel(q_ref, k_ref, v_ref, km_ref, vm_ref, o_ref, c_scr, acc_scr, *, nq, npair):
    nchain = 2 * npair
    lane = lax.broadcasted_iota(jnp.int32, (TQ, PAIR), 1)
    low = lane < HEAD_DIM
    row = lax.broadcasted_iota(jnp.int32, (TQ, TQ), 0)
    col = lax.broadcasted_iota(jnp.int32, (TQ, TQ), 1)
    tmat = (row > col).astype(BF16)
    diag_mask = col < row
    meta_mask = lax.broadcasted_iota(jnp.int32, (TQ, LANES), 1) < N_META

    def cols(p):
        return slice(p * PAIR, (p + 1) * PAIR)

    def load_q(r0):
        qs = []
        for p in range(npair):
            q2 = q_ref[pl.ds(r0, TQ), cols(p)]
            zero = jnp.zeros_like(q2)
            qs += [jnp.where(low, q2, zero), jnp.where(low, zero, q2)]
        return qs

    def kv_tile(qs, k0, mask, state):
        items = []
        for i in range(nchain):
            kb = k_ref[pl.ds(k0, TQ), cols(i // 2)]
            vb = v_ref[pl.ds(k0, TQ), cols(i // 2)]
            if state == "scratch":
                items.append((qs[i], kb, vb, mask, c_scr[i], acc_scr[i], None))
            elif state == "fresh":
                items.append((qs[i], kb, vb, mask, None, None, None))
            else:
                items.append((qs[i], kb, vb, mask, None, None, state + i))
        return items

    def commit(cs, accs):
        for i in range(nchain):
            c_scr[i] = cs[i]
            acc_scr[i] = accs[i]

    def unfinished():
        m = c_scr[0]
        for i in range(1, nchain):
            m = jnp.minimum(m, c_scr[i])
        return (jnp.min(m) < C_SKIP).astype(jnp.int32)

    def meta_tile(qs):
        items = [(qs[i], km_ref[:, cols(i // 2)], vm_ref[:, cols(i // 2)], meta_mask,
                  c_scr[i], acc_scr[i], None) for i in range(nchain)]
        commit(*_sb_tiles(items, tmat))

    def write_out(r0):
        for p in range(npair):
            o_ref[pl.ds(r0, TQ), cols(p)] = jnp.where(low, acc_scr[2 * p], acc_scr[2 * p + 1]).astype(BF16)

    qs0 = load_q(0)
    commit(*_sb_tiles(kv_tile(qs0, 0, diag_mask, "fresh"), tmat))
    meta_tile(qs0)
    write_out(0)

    def qblock(qi, carry):
        r0 = pl.multiple_of(qi * TQ, TQ)
        qs = load_q(r0)
        items = (kv_tile(qs, r0, diag_mask, "fresh")
                 + kv_tile(qs, pl.multiple_of(r0 - TQ, TQ), None, 0))
        cs, accs = _sb_tiles(items, tmat)
        commit(cs[nchain:], accs[nchain:])

        def cond(st):
            return jnp.logical_and(st[0] < qi, st[1] > 0)

        def body(st):
            k0 = pl.multiple_of((qi - 1 - st[0]) * TQ, TQ)
            commit(*_sb_tiles(kv_tile(qs, k0, None, "scratch"), tmat))
            return st[0] + 1, unfinished()

        _, more = lax.while_loop(cond, body, (jnp.int32(1), unfinished()))

        @pl.when(more > 0)
        def _():
            meta_tile(qs)

        write_out(r0)
        return carry

    lax.fori_loop(1, nq, qblock, 0)


def _attn(proj, kmeta, vmeta, *, nbatch, seq, npair=4):
    width = npair * PAIR
    nblk = SB_WIDTH // width
    return pl.pallas_call(
        functools.partial(_attn_kernel, nq=seq // TQ, npair=npair),
        grid=(nbatch, nblk),
        in_specs=[
            pl.BlockSpec((seq, width), lambda b, p: (b, COL_Q * nblk + p)),
            pl.BlockSpec((seq, width), lambda b, p: (b, COL_K * nblk + p)),
            pl.BlockSpec((seq, width), lambda b, p: (b, COL_V * nblk + p)),
            pl.BlockSpec((LANES, width), lambda b, p: (0, p)),
            pl.BlockSpec((LANES, width), lambda b, p: (0, p)),
        ],
        out_specs=pl.BlockSpec((seq, width), lambda b, p: (b, p)),
        out_shape=jax.ShapeDtypeStruct((nbatch * seq, SB_WIDTH), BF16),
        scratch_shapes=[pltpu.VMEM((2 * npair, TQ, 1), F32), pltpu.VMEM((2 * npair, TQ, PAIR), F32)],
        compiler_params=pltpu.CompilerParams(dimension_semantics=("arbitrary", "arbitrary")),
        name="sb_attn",
    )(proj, proj, proj, kmeta, vmeta)


def _outproj_kernel(x_ref, osb_ref, gate_ref, yssd_ref, sbw_ref, wo_ref, fw_ref, o_ref, *, tm, sub):
    nsub = tm // sub

    def gated_norm(s):
        rows = slice(s * sub, (s + 1) * sub)
        ys = osb_ref[rows, :].astype(F32) * gate_ref[rows, :].astype(F32)
        return (ys * lax.rsqrt(jnp.mean(ys * ys, axis=-1, keepdims=True) + EPS) * sbw_ref[...]).astype(BF16)

    ysb = gated_norm(0)
    for s in range(nsub):
        rows = slice(s * sub, (s + 1) * sub)
        acc = (_dot(ysb, wo_ref[0:SB_WIDTH, :])
               + _dot(yssd_ref[rows, :], wo_ref[SB_WIDTH:SB_WIDTH + SSD_WIDTH, :]))
        if s + 1 < nsub:
            ysb = gated_norm(s + 1)
        h = x_ref[rows, :] + acc
        o_ref[rows, :] = h * lax.rsqrt(jnp.mean(h * h, axis=-1, keepdims=True) + EPS) * fw_ref[...]


def _outproj(x2d, osb, proj, yssd, sb_norm_w, w_out, final_w, *, tm=1024):
    rows = x2d.shape[0]
    return pl.pallas_call(
        functools.partial(_outproj_kernel, tm=tm, sub=256),
        grid=(rows // tm,),
        in_specs=[
            pl.BlockSpec((tm, D_MODEL), lambda i: (i, 0)),
            pl.BlockSpec((tm, SB_WIDTH), lambda i: (i, 0)),
            pl.BlockSpec((tm, SB_WIDTH), lambda i: (i, COL_GATE)),
            pl.BlockSpec((tm, SSD_WIDTH), lambda i: (i, 0)),
            pl.BlockSpec((1, SB_WIDTH), lambda i: (0, 0)),
            pl.BlockSpec((SB_WIDTH + SSD_WIDTH, D_MODEL), lambda i: (0, 0)),
            pl.BlockSpec((1, D_MODEL), lambda i: (0, 0)),
        ],
        out_specs=pl.BlockSpec((tm, D_MODEL), lambda i: (i, 0)),
        out_shape=jax.ShapeDtypeStruct((rows, D_MODEL), F32),
        compiler_params=pltpu.CompilerParams(dimension_semantics=("arbitrary",)),
        name="outproj",
    )(x2d, osb, proj, yssd, sb_norm_w, w_out, final_w)


def _layer(x2d, meta, norm_w, w_in, conv_w, conv_b, dt_bias, a_log, d_skip, sb_norm_w,
           ssd_norm_w, w_out, nbatch, seq):
    nh = SSD_HEADS
    w_main = w_in.astype(BF16)
    w_dt = jnp.pad(w_in[:, D_MAIN:], ((0, 0), (0, LANES - nh))).astype(BF16)
    dtb = jnp.pad(dt_bias, (0, LANES - nh)).reshape(1, LANES)
    alog = jnp.pad(a_log, (0, LANES - nh)).reshape(1, LANES)
    norm_w = norm_w.reshape(1, D_MODEL)
    dsk_rep = jnp.repeat(d_skip, HEAD_DIM).reshape(1, SSD_WIDTH)
    e_mat = (jnp.arange(LANES)[:, None] == (jnp.arange(SSD_WIDTH)[None, :] // HEAD_DIM)).astype(BF16)
    conv_b = conv_b.reshape(1, SSD_XBC)
    ssd_norm_w = ssd_norm_w.reshape(1, SSD_WIDTH)

    zeros_t = jnp.zeros((8, SSD_XBC), F32)
    proj_m, dt_m, tail_meta = _inproj(meta, norm_w, w_main, w_dt, dtb, zeros_t, conv_w, conv_b,
                                      tm=N_META, rows_per_seq=N_META, emit_tail=True)
    pad = CHUNK - N_META
    proj_mp = jnp.pad(proj_m, ((pad, 0), (0, 0)))
    dt_mp = jnp.pad(dt_m, ((pad, 0), (0, 0)))
    zeros_s = jnp.zeros((SSD_GROUPS, SSD_STATE, 512), F32)
    _, s_meta = _ssd(proj_mp, dt_mp, zeros_s, alog, dsk_rep, ssd_norm_w, e_mat,
                     nbatch=1, ts=CHUNK, emit_state=True)
    kmeta = jnp.pad(proj_m[:, COL_K * 1024:(COL_K + 1) * 1024], ((0, LANES - N_META), (0, 0)))
    vmeta = jnp.pad(proj_m[:, COL_V * 1024:(COL_V + 1) * 1024], ((0, LANES - N_META), (0, 0)))

    proj, dt = _inproj(x2d, norm_w, w_main, w_dt, dtb, tail_meta, conv_w, conv_b,
                       tm=512, rows_per_seq=seq, emit_tail=False)
    (yssd,) = _ssd(proj, dt, s_meta, alog, dsk_rep, ssd_norm_w, e_mat,
                   nbatch=nbatch, ts=TQ, emit_state=False)
    osb = _attn(proj, kmeta, vmeta, nbatch=nbatch, seq=seq)
    return osb, proj, yssd


def kernel(x, meta_tokens, norm_w, w_in, conv_w, conv_b, dt_bias, a_log, d_skip, sb_norm_w,
           ssd_norm_w, w_out, final_norm_w):
    nbatch, seq, _ = x.shape
    assert norm_w.shape[0] == 1, "single-layer block"
    x2d = x.reshape(nbatch * seq, D_MODEL)
    osb, proj, yssd = _layer(x2d, meta_tokens, norm_w[0], w_in[0], conv_w[0], conv_b[0], dt_bias[0],
                             a_log[0], d_skip[0], sb_norm_w[0], ssd_norm_w[0], w_out[0], nbatch, seq)
    out = _outproj(x2d, osb, proj, yssd, sb_norm_w[0].reshape(1, SB_WIDTH), w_out[0].astype(BF16),
                   final_norm_w.reshape(1, D_MODEL))
    return out.reshape(nbatch, seq, D_MODEL)
```

```python
import functools
import math

import jax
import jax.numpy as jnp
from jax import lax
from jax.experimental import pallas as pl
from jax.experimental.pallas import tpu as pltpu

F32 = jnp.float32
BF16 = jnp.bfloat16

D_MODEL = 1024
N_META = 16
HEAD_DIM = 64
SB_WIDTH = 1024
SSD_WIDTH = 1024
SSD_HEADS = 16
SSD_GROUPS = 2
SSD_STATE = 128
SSD_CONV = 4
SSD_XBC = SSD_WIDTH + 2 * SSD_GROUPS * SSD_STATE
D_MAIN = 4 * SB_WIDTH + SSD_WIDTH + SSD_XBC
EPS = 1e-5

LANES = 128
PAIR = 2 * HEAD_DIM
CHUNK = 64
TQ = 256
NEG_BIG = -1e30
Q_SCALE = 1.0 / math.sqrt(HEAD_DIM)
C_SKIP = 111.0

COL_Q, COL_K, COL_V, COL_GATE, COL_Z, COL_XS = 0, 1, 2, 3, 4, 5
COL_B, COL_C = 24, 25
GATE0 = COL_GATE * 1024
XBC0 = COL_XS * 1024


def _softplus(x):
    return jnp.maximum(x, 0.0) + jnp.log(1.0 + jnp.exp(-jnp.abs(x)))


def _silu(x):
    h = 0.5 * x
    return h + h * jnp.tanh(h)


def _split3(x):
    hi = x.astype(BF16)
    r1 = x - hi.astype(F32)
    mid = r1.astype(BF16)
    lo = (r1 - mid.astype(F32)).astype(BF16)
    return hi, mid, lo


def _dot(a, b):
    return jnp.dot(a, b, preferred_element_type=F32)


def _dot_nt(a, b):
    return lax.dot_general(a, b, (((1,), (1,)), ((), ())), preferred_element_type=F32)


def _dot_tn(a, b):
    return lax.dot_general(a, b, (((0,), (0,)), ((), ())), preferred_element_type=F32)


def _inproj_kernel(x_ref, nw_ref, w_ref, wdt_ref, dtb_ref, tail0_ref, cw_ref, cb_ref, *rest,
                   tm, sub, tn, tiles_per_seq, emit_tail):
    if emit_tail:
        proj_ref, dt_ref, tail_ref, ext_scr = rest
    else:
        proj_ref, dt_ref, ext_scr = rest

    @pl.when(pl.program_id(0) % tiles_per_seq == 0)
    def _():
        ext_scr[0:8, :] = tail0_ref[...]

    for r0 in range(0, tm, sub):
        rows = slice(r0, r0 + sub)
        x = x_ref[rows, :]
        u = x * lax.rsqrt(jnp.mean(x * x, axis=-1, keepdims=True) + EPS) * nw_ref[...]
        ub = u.astype(BF16)
        dt_ref[rows, :] = _softplus(_dot(ub, wdt_ref[...]) + dtb_ref[...])
        for c0 in range(XBC0, D_MAIN, tn):
            cc = slice(c0 - XBC0, c0 - XBC0 + tn)
            ext_scr[8:8 + sub, cc] = _dot(ub, w_ref[:, c0:c0 + tn])
            ea = ext_scr[:, cc]
            e1 = pltpu.roll(ea, 1, axis=0)
            pa = cw_ref[3:4, cc] * ea + cw_ref[2:3, cc] * e1
            pb = cw_ref[1:2, cc] * ea + cw_ref[0:1, cc] * e1
            conv = cb_ref[:, cc] + pa[8:8 + sub] + pb[6:6 + sub]
            proj_ref[rows, c0:c0 + tn] = _silu(conv).astype(BF16)
            ext_scr[0:8, cc] = ext_scr[sub:sub + 8, cc]
        for c0 in list(range(GATE0, XBC0, tn)) + list(range(0, GATE0, tn)):
            acc = _dot(ub, w_ref[:, c0:c0 + tn])
            if c0 < SB_WIDTH:
                acc = acc * Q_SCALE
            elif c0 >= GATE0:
                acc = _silu(acc)
            proj_ref[rows, c0:c0 + tn] = acc.astype(BF16)
    if emit_tail:
        tail_ref[...] = ext_scr[0:8, :]


def _inproj(x2d, norm_w, w_all, w_dt, dt_bias, tail0, conv_w, conv_b, *, tm, rows_per_seq, emit_tail, tn=512):
    rows = x2d.shape[0]
    sub = min(tm, 256)
    assert SB_WIDTH % tn == 0 and GATE0 % tn == 0 and XBC0 % tn == 0 and D_MAIN % tn == 0
    assert rows_per_seq % tm == 0 and tm % sub == 0
    const = lambda i: (0, 0)
    out_specs = [pl.BlockSpec((tm, D_MAIN), lambda i: (i, 0)), pl.BlockSpec((tm, LANES), lambda i: (i, 0))]
    out_shape = [jax.ShapeDtypeStruct((rows, D_MAIN), BF16), jax.ShapeDtypeStruct((rows, LANES), F32)]
    if emit_tail:
        out_specs.append(pl.BlockSpec((8, SSD_XBC), const))
        out_shape.append(jax.ShapeDtypeStruct((8, SSD_XBC), F32))
    return pl.pallas_call(
        functools.partial(_inproj_kernel, tm=tm, sub=sub, tn=tn, tiles_per_seq=rows_per_seq // tm,
                          emit_tail=emit_tail),
        grid=(rows // tm,),
        in_specs=[
            pl.BlockSpec((tm, D_MODEL), lambda i: (i, 0)),
            pl.BlockSpec((1, D_MODEL), const),
            pl.BlockSpec((D_MODEL, D_MAIN), const, pipeline_mode=pl.Buffered(1)),
            pl.BlockSpec((D_MODEL, LANES), const),
            pl.BlockSpec((1, LANES), const),
            pl.BlockSpec((8, SSD_XBC), const),
            pl.BlockSpec((SSD_CONV, SSD_XBC), const),
            pl.BlockSpec((1, SSD_XBC), const),
        ],
        out_specs=out_specs,
        out_shape=out_shape,
        scratch_shapes=[pltpu.VMEM((sub + 8, SSD_XBC), F32)],
        compiler_params=pltpu.CompilerParams(dimension_semantics=("arbitrary",)),
        name="inproj",
    )(x2d, norm_w, w_all, w_dt, dt_bias, tail0, conv_w, conv_b)


def _ssd_kernel(zact_ref, xs_ref, b_ref, c_ref, dt_ref, s0_ref, alog_ref, dsk_ref, nw_ref, e_ref,
                *rest, ts, emit_state):
    if emit_state:
        y_ref, sfin_ref, s_scr, y_scr = rest
    else:
        y_ref, s_scr, y_scr = rest

    @pl.when(pl.program_id(1) == 0)
    def _():
        s_scr[...] = s0_ref[...]

    nchunk = ts // CHUNK
    a_row = -jnp.exp(alog_ref[...])
    li = lax.broadcasted_iota(jnp.int32, (ts, ts), 0)
    si = lax.broadcasted_iota(jnp.int32, (ts, ts), 1)
    t_incl = jnp.logical_and(si <= li, si // CHUNK == li // CHUNK).astype(BF16)
    l2 = lax.broadcasted_iota(jnp.int32, (CHUNK, LANES), 0)
    lane2 = lax.broadcasted_iota(jnp.int32, (CHUNK, LANES), 1)
    mask2 = (lane2 % CHUNK) <= l2
    low_half = lane2 < HEAD_DIM
    e_mat = e_ref[...]

    xs = xs_ref[...].astype(F32)
    dt = dt_ref[...]
    h3 = _split3(dt * a_row)
    acum = _dot(t_incl, h3[0]) + _dot(t_incl, h3[1]) + _dot(t_incl, h3[2])
    st3 = _split3(jnp.concatenate([dt, acum], axis=0))
    e2 = jnp.concatenate([e_mat, e_mat], axis=0)
    rep = _dot(jnp.concatenate([st3[0], st3[1]], axis=1), e2)
    dtr = rep[0:ts]
    acr = rep[ts:2 * ts] + _dot(st3[2][ts:2 * ts], e_mat)
    xdt = xs * dtr
    ebase = jnp.exp(acr)

    chunks = [slice(c * CHUNK, (c + 1) * CHUNK) for c in range(nchunk)]

    yoffs = []
    for c, r in enumerate(chunks):
        atot = acr[(c + 1) * CHUNK - 1:(c + 1) * CHUNK, :]
        xw = (xdt[r] * jnp.exp(atot - acr[r])).astype(BF16)
        etot = jnp.exp(atot)
        for g in range(SSD_GROUPS):
            gs = slice(g * 512, (g + 1) * 512)
            ns = slice(g * SSD_STATE, (g + 1) * SSD_STATE)
            sg = s_scr[g]
            yoffs.append(_dot(c_ref[r, ns], sg.astype(BF16)) * ebase[r, gs])
            s_scr[g] = sg * etot[:, gs] + _dot_tn(b_ref[r, ns], xw[:, gs])

    wts = [jnp.concatenate([acum[r], pltpu.roll(acum[r], LANES - 1, axis=1)], axis=0).T for r in chunks]
    cb2s = []
    for r in chunks:
        for g in range(SSD_GROUPS):
            ns = slice(g * SSD_STATE, (g + 1) * SSD_STATE)
            bg = b_ref[r, ns]
            cb2s.append(_dot_nt(c_ref[r, ns], jnp.concatenate([bg, bg], axis=0)))
    npairs = SSD_HEADS // 2
    m2s, xblks = [], []
    for c, r in enumerate(chunks):
        for p in range(npairs):
            cols = slice(p * PAIR, (p + 1) * PAIR)
            seg = acr[r, cols] - wts[c][2 * p:2 * p + 1, :]
            m2s.append((cb2s[c * SSD_GROUPS + p // 4] * jnp.exp(jnp.where(mask2, seg, NEG_BIG))).astype(BF16))
            x2 = xdt[r, cols].astype(BF16)
            zero = jnp.zeros_like(x2)
            xblks.append(jnp.concatenate([jnp.where(low_half, x2, zero),
                                          jnp.where(low_half, zero, x2)], axis=0))
    ydiags = [_dot(m2, xblk) for m2, xblk in zip(m2s, xblks)]
    for c, r in enumerate(chunks):
        for p in range(npairs):
            cols = slice(p * PAIR, (p + 1) * PAIR)
            yoff = yoffs[c * SSD_GROUPS + p // 4]
            y_scr[r, cols] = (ydiags[c * npairs + p] + yoff[:, (p % 4) * PAIR:(p % 4 + 1) * PAIR]
                              + xs[r, cols] * dsk_ref[:, cols])

    yg = y_scr[...] * zact_ref[...].astype(F32)
    y_ref[...] = (yg * lax.rsqrt(jnp.mean(yg * yg, axis=-1, keepdims=True) + EPS)
                  * nw_ref[...]).astype(BF16)
    if emit_state:
        sfin_ref[...] = s_scr[...]


def _ssd(proj, dt, s0, a_log, dsk_rep, norm_w, e_mat, *, nbatch, ts, emit_state):
    rows = proj.shape[0]
    nsteps = rows // (nbatch * ts)
    rb = lambda b, s: b * nsteps + s
    const = lambda b, s: (0, 0)
    in_specs = [
        pl.BlockSpec((ts, 1024), lambda b, s: (rb(b, s), COL_Z)),
        pl.BlockSpec((ts, 1024), lambda b, s: (rb(b, s), COL_XS)),
        pl.BlockSpec((ts, 256), lambda b, s: (rb(b, s), COL_B)),
        pl.BlockSpec((ts, 256), lambda b, s: (rb(b, s), COL_C)),
        pl.BlockSpec((ts, LANES), lambda b, s: (rb(b, s), 0)),
        pl.BlockSpec((SSD_GROUPS, SSD_STATE, 512), lambda b, s: (0, 0, 0)),
        pl.BlockSpec((1, LANES), const),
        pl.BlockSpec((1, SSD_WIDTH), const),
        pl.BlockSpec((1, SSD_WIDTH), const),
        pl.BlockSpec((LANES, SSD_WIDTH), const),
    ]
    out_specs = [pl.BlockSpec((ts, SSD_WIDTH), lambda b, s: (rb(b, s), 0))]
    out_shape = [jax.ShapeDtypeStruct((rows, SSD_WIDTH), BF16)]
    if emit_state:
        out_specs.append(pl.BlockSpec((SSD_GROUPS, SSD_STATE, 512), lambda b, s: (0, 0, 0)))
        out_shape.append(jax.ShapeDtypeStruct((SSD_GROUPS, SSD_STATE, 512), F32))
    return pl.pallas_call(
        functools.partial(_ssd_kernel, ts=ts, emit_state=emit_state),
        grid=(nbatch, nsteps),
        in_specs=in_specs,
        out_specs=out_specs,
        out_shape=out_shape,
        scratch_shapes=[
            pltpu.VMEM((SSD_GROUPS, SSD_STATE, 512), F32),
            pltpu.VMEM((ts, SSD_WIDTH), F32),
        ],
        compiler_params=pltpu.CompilerParams(dimension_semantics=("arbitrary", "arbitrary")),
        name="ssd_meta" if emit_state else "ssd",
    )(proj, proj, proj, proj, dt, s0, a_log, dsk_rep, norm_w, e_mat)


def _sb_tiles(items, tmat):
    n = len(items)
    ys = [_dot_nt(it[0], it[1]).astype(BF16) for it in items]
    sps, owns = [], []
    for y, it in zip(ys, items):
        lg = jnp.log(1.0 + jnp.exp(-jnp.abs(y)))
        sp = jnp.maximum(y, 0.0) + lg
        owns.append(jnp.minimum(y, 0.0) - lg)
        if it[3] is not None:
            sp = jnp.where(it[3], sp, jnp.zeros_like(sp))
        sps.append(sp)
    excls = [_dot(sp, tmat[0:sp.shape[1], 0:sp.shape[1]]) for sp in sps]
    rss = [excls[i][:, 0:1] + sps[i][:, 0:1].astype(F32) for i in range(n)]
    c_in, c_out = [], []
    for i, it in enumerate(items):
        c = c_out[it[6]] if it[6] is not None else it[4]
        c_in.append(c)
        c_out.append(rss[i] if c is None else c + rss[i])
    ws = []
    for i, it in enumerate(items):
        t = excls[i] if c_in[i] is None else excls[i] + c_in[i]
        w = jnp.exp(owns[i] - t.astype(BF16))
        if it[3] is not None:
            w = jnp.where(it[3], w, jnp.zeros_like(w))
        ws.append(w)
    pvs = [_dot(ws[i], items[i][2]) for i in range(n)]
    a_out = []
    for i, it in enumerate(items):
        a = a_out[it[6]] if it[6] is not None else it[5]
        a_out.append(pvs[i] if a is None else a + pvs[i])
    return c_out, a_out


def _attn_kernel(q_ref, k_ref, v_ref, km_ref, vm_ref, o_ref, c_scr, acc_scr, *, nq, npair):
    nchain = 2 * npair
    HALF = TQ // 2
    lane = lax.broadcasted_iota(jnp.int32, (TQ, PAIR), 1)
    low = lane < HEAD_DIM
    row = lax.broadcasted_iota(jnp.int32, (TQ, TQ), 0)
    col = lax.broadcasted_iota(jnp.int32, (TQ, TQ), 1)
    tmat = (row > col).astype(BF16)
    diag_mask = col < row
    meta_mask = lax.broadcasted_iota(jnp.int32, (TQ, LANES), 1) < N_META

    def cols(p):
        return slice(p * PAIR, (p + 1) * PAIR)

    def load_q(r0):
        qs = []
        for p in range(npair):
            q2 = q_ref[pl.ds(r0, TQ), cols(p)]
            zero = jnp.zeros_like(q2)
            qs += [jnp.where(low, q2, zero), jnp.where(low, zero, q2)]
        return qs

    def kv_tile(qs, k0, mask, state):
        items = []
        for i in range(nchain):
            kb = k_ref[pl.ds(k0, TQ), cols(i // 2)]
            vb = v_ref[pl.ds(k0, TQ), cols(i // 2)]
            if state == "scratch":
                items.append((qs[i], kb, vb, mask, c_scr[i], acc_scr[i], None))
            elif state == "fresh":
                items.append((qs[i], kb, vb, mask, None, None, None))
            else:
                items.append((qs[i], kb, vb, mask, None, None, state + i))
        return items

    halves = (slice(0, HALF), slice(HALF, TQ))

    def diag_items(qs, r0):
        items = []
        for h, rws in enumerate(halves):
            nk = HALF if h == 0 else TQ
            for i in range(nchain):
                items.append((qs[i][rws], k_ref[pl.ds(r0, nk), cols(i // 2)],
                              v_ref[pl.ds(r0, nk), cols(i // 2)], diag_mask[rws, 0:nk], None, None, None))
        return items

    def commit(cs, accs):
        for i in range(nchain):
            c_scr[i] = cs[i]
            acc_scr[i] = accs[i]

    def commit_halves(cs, accs):
        for h, rws in enumerate(halves):
            for i in range(nchain):
                c_scr[i, rws, :] = cs[h * nchain + i]
                acc_scr[i, rws, :] = accs[h * nchain + i]

    def unfinished():
        m = c_scr[0]
        for i in range(1, nchain):
            m = jnp.minimum(m, c_scr[i])
        return (jnp.min(m) < C_SKIP).astype(jnp.int32)

    def meta_tile(qs):
        items = [(qs[i], km_ref[:, cols(i // 2)], vm_ref[:, cols(i // 2)], meta_mask,
                  c_scr[i], acc_scr[i], None) for i in range(nchain)]
        commit(*_sb_tiles(items, tmat))

    def write_out(r0):
        for p in range(npair):
            o_ref[pl.ds(r0, TQ), cols(p)] = jnp.where(low, acc_scr[2 * p], acc_scr[2 * p + 1]).astype(BF16)

    qs0 = load_q(0)
    commit_halves(*_sb_tiles(diag_items(qs0, 0), tmat))
    meta_tile(qs0)
    write_out(0)

    def qblock(qi, carry):
        r0 = pl.multiple_of(qi * TQ, TQ)
        qs = load_q(r0)
        kp = pl.multiple_of(r0 - TQ, TQ)
        items = diag_items(qs, r0)
        for h, rws in enumerate(halves):
            for i in range(nchain):
                items.append((qs[i][rws], k_ref[pl.ds(kp, TQ), cols(i // 2)],
                              v_ref[pl.ds(kp, TQ), cols(i // 2)], None, None, None, h * nchain + i))
        cs, accs = _sb_tiles(items, tmat)
        commit_halves(cs[2 * nchain:], accs[2 * nchain:])

        def cond(st):
            return jnp.logical_and(st[0] < qi, st[1] > 0)

        def body(st):
            k0 = pl.multiple_of((qi - 1 - st[0]) * TQ, TQ)
            commit(*_sb_tiles(kv_tile(qs, k0, None, "scratch"), tmat))
            return st[0] + 1, unfinished()

        _, more = lax.while_loop(cond, body, (jnp.int32(1), unfinished()))

        @pl.when(more > 0)
        def _():
            meta_tile(qs)

        write_out(r0)
        return carry

    lax.fori_loop(1, nq, qblock, 0)


def _attn(proj, kmeta, vmeta, *, nbatch, seq, npair=4):
    width = npair * PAIR
    nblk = SB_WIDTH // width
    return pl.pallas_call(
        functools.partial(_attn_kernel, nq=seq // TQ, npair=npair),
        grid=(nbatch, nblk),
        in_specs=[
            pl.BlockSpec((seq, width), lambda b, p: (b, COL_Q * nblk + p)),
            pl.BlockSpec((seq, width), lambda b, p: (b, COL_K * nblk + p)),
            pl.BlockSpec((seq, width), lambda b, p: (b, COL_V * nblk + p)),
            pl.BlockSpec((LANES, width), lambda b, p: (0, p)),
            pl.BlockSpec((LANES, width), lambda b, p: (0, p)),
        ],
        out_specs=pl.BlockSpec((seq, width), lambda b, p: (b, p)),
        out_shape=jax.ShapeDtypeStruct((nbatch * seq, SB_WIDTH), BF16),
        scratch_shapes=[pltpu.VMEM((2 * npair, TQ, 1), F32), pltpu.VMEM((2 * npair, TQ, PAIR), F32)],
        compiler_params=pltpu.CompilerParams(dimension_semantics=("arbitrary", "arbitrary")),
        name="sb_attn",
    )(proj, proj, proj, kmeta, vmeta)


def _outproj_kernel(x_ref, osb_ref, gate_ref, yssd_ref, sbw_ref, wo_ref, fw_ref, o_ref, *, tm, sub):
    nsub = tm // sub

    def gated_norm(s):
        rows = slice(s * sub, (s + 1) * sub)
        ys = osb_ref[rows, :].astype(F32) * gate_ref[rows, :].astype(F32)
        return (ys * lax.rsqrt(jnp.mean(ys * ys, axis=-1, keepdims=True) + EPS) * sbw_ref[...]).astype(BF16)

    ysb = gated_norm(0)
    for s in range(nsub):
        rows = slice(s * sub, (s + 1) * sub)
        acc = (_dot(ysb, wo_ref[0:SB_WIDTH, :])
               + _dot(yssd_ref[rows, :], wo_ref[SB_WIDTH:SB_WIDTH + SSD_WIDTH, :]))
        if s + 1 < nsub:
            ysb = gated_norm(s + 1)
        h = x_ref[rows, :] + acc
        o_ref[rows, :] = h * lax.rsqrt(jnp.mean(h * h, axis=-1, keepdims=True) + EPS) * fw_ref[...]


def _outproj(x2d, osb, proj, yssd, sb_norm_w, w_out, final_w, *, tm=1024):
    rows = x2d.shape[0]
    return pl.pallas_call(
        functools.partial(_outproj_kernel, tm=tm, sub=256),
        grid=(rows // tm,),
        in_specs=[
            pl.BlockSpec((tm, D_MODEL), lambda i: (i, 0)),
            pl.BlockSpec((tm, SB_WIDTH), lambda i: (i, 0)),
            pl.BlockSpec((tm, SB_WIDTH), lambda i: (i, COL_GATE)),
            pl.BlockSpec((tm, SSD_WIDTH), lambda i: (i, 0)),
            pl.BlockSpec((1, SB_WIDTH), lambda i: (0, 0)),
            pl.BlockSpec((SB_WIDTH + SSD_WIDTH, D_MODEL), lambda i: (0, 0)),
            pl.BlockSpec((1, D_MODEL), lambda i: (0, 0)),
        ],
        out_specs=pl.BlockSpec((tm, D_MODEL), lambda i: (i, 0)),
        out_shape=jax.ShapeDtypeStruct((rows, D_MODEL), F32),
        compiler_params=pltpu.CompilerParams(dimension_semantics=("arbitrary",)),
        name="outproj",
    )(x2d, osb, proj, yssd, sb_norm_w, w_out, final_w)


def _layer(x2d, meta, norm_w, w_in, conv_w, conv_b, dt_bias, a_log, d_skip, sb_norm_w,
           ssd_norm_w, w_out, nbatch, seq):
    nh = SSD_HEADS
    w_main = w_in.astype(BF16)
    w_dt = jnp.pad(w_in[:, D_MAIN:], ((0, 0), (0, LANES - nh))).astype(BF16)
    dtb = jnp.pad(dt_bias, (0, LANES - nh)).reshape(1, LANES)
    alog = jnp.pad(a_log, (0, LANES - nh)).reshape(1, LANES)
    norm_w = norm_w.reshape(1, D_MODEL)
    dsk_rep = jnp.repeat(d_skip, HEAD_DIM).reshape(1, SSD_WIDTH)
    e_mat = (jnp.arange(LANES)[:, None] == (jnp.arange(SSD_WIDTH)[None, :] // HEAD_DIM)).astype(BF16)
    conv_b = conv_b.reshape(1, SSD_XBC)
    ssd_norm_w = ssd_norm_w.reshape(1, SSD_WIDTH)

    zeros_t = jnp.zeros((8, SSD_XBC), F32)
    proj_m, dt_m, tail_meta = _inproj(meta, norm_w, w_main, w_dt, dtb, zeros_t, conv_w, conv_b,
                                      tm=N_META, rows_per_seq=N_META, emit_tail=True)
    pad = CHUNK - N_META
    proj_mp = jnp.pad(proj_m, ((pad, 0), (0, 0)))
    dt_mp = jnp.pad(dt_m, ((pad, 0), (0, 0)))
    zeros_s = jnp.zeros((SSD_GROUPS, SSD_STATE, 512), F32)
    _, s_meta = _ssd(proj_mp, dt_mp, zeros_s, alog, dsk_rep, ssd_norm_w, e_mat,
                     nbatch=1, ts=CHUNK, emit_state=True)
    kmeta = jnp.pad(proj_m[:, COL_K * 1024:(COL_K + 1) * 1024], ((0, LANES - N_META), (0, 0)))
    vmeta = jnp.pad(proj_m[:, COL_V * 1024:(COL_V + 1) * 1024], ((0, LANES - N_META), (0, 0)))

    proj, dt = _inproj(x2d, norm_w, w_main, w_dt, dtb, tail_meta, conv_w, conv_b,
                       tm=512, rows_per_seq=seq, emit_tail=False)
    (yssd,) = _ssd(proj, dt, s_meta, alog, dsk_rep, ssd_norm_w, e_mat,
                   nbatch=nbatch, ts=TQ, emit_state=False)
    osb = _attn(proj, kmeta, vmeta, nbatch=nbatch, seq=seq)
    return osb, proj, yssd


def kernel(x, meta_tokens, norm_w, w_in, conv_w, conv_b, dt_bias, a_log, d_skip, sb_norm_w,
           ssd_norm_w, w_out, final_norm_w):
    nbatch, seq, _ = x.shape
    assert norm_w.shape[0] == 1, "single-layer block"
    x2d = x.reshape(nbatch * seq, D_MODEL)
    osb, proj, yssd = _layer(x2d, meta_tokens, norm_w[0], w_in[0], conv_w[0], conv_b[0], dt_bias[0],
                             a_log[0], d_skip[0], sb_norm_w[0], ssd_norm_w[0], w_out[0], nbatch, seq)
    out = _outproj(x2d, osb, proj, yssd, sb_norm_w[0].reshape(1, SB_WIDTH), w_out[0].astype(BF16),
                   final_norm_w.reshape(1, D_MODEL))
    return out.reshape(nbatch, seq, D_MODEL)
```

```python
import functools
import math

import jax
import jax.numpy as jnp
from jax import lax
from jax.experimental import pallas as pl
from jax.experimental.pallas import tpu as pltpu

F32 = jnp.float32
BF16 = jnp.bfloat16

D_MODEL = 1024
N_META = 16
HEAD_DIM = 64
SB_WIDTH = 1024
SSD_WIDTH = 1024
SSD_HEADS = 16
SSD_GROUPS = 2
SSD_STATE = 128
SSD_CONV = 4
SSD_XBC = SSD_WIDTH + 2 * SSD_GROUPS * SSD_STATE
D_MAIN = 4 * SB_WIDTH + SSD_WIDTH + SSD_XBC
EPS = 1e-5

LANES = 128
PAIR = 2 * HEAD_DIM
CHUNK = 64
TQ = 256
NEG_BIG = -1e30
Q_SCALE = 1.0 / math.sqrt(HEAD_DIM)
C_SKIP = 111.0

COL_Q, COL_K, COL_V, COL_GATE, COL_Z, COL_XS = 0, 1, 2, 3, 4, 5
COL_B, COL_C = 24, 25
GATE0 = COL_GATE * 1024
XBC0 = COL_XS * 1024


def _softplus(x):
    return jnp.maximum(x, 0.0) + jnp.log(1.0 + jnp.exp(-jnp.abs(x)))


def _silu(x):
    h = 0.5 * x
    return h + h * jnp.tanh(h)


def _split3(x):
    hi = x.astype(BF16)
    r1 = x - hi.astype(F32)
    mid = r1.astype(BF16)
    lo = (r1 - mid.astype(F32)).astype(BF16)
    return hi, mid, lo


def _dot(a, b):
    return jnp.dot(a, b, preferred_element_type=F32)


def _dot_nt(a, b):
    return lax.dot_general(a, b, (((1,), (1,)), ((), ())), preferred_element_type=F32)


def _dot_tn(a, b):
    return lax.dot_general(a, b, (((0,), (0,)), ((), ())), preferred_element_type=F32)


def _inproj_kernel(x_ref, nw_ref, w_ref, wdt_ref, dtb_ref, tail0_ref, cw_ref, cb_ref, *rest,
                   tm, sub, tn, tiles_per_seq, emit_tail):
    if emit_tail:
        proj_ref, dt_ref, tail_ref, ext_scr = rest
    else:
        proj_ref, dt_ref, ext_scr = rest

    @pl.when(pl.program_id(0) % tiles_per_seq == 0)
    def _():
        ext_scr[0:8, :] = tail0_ref[...]

    for r0 in range(0, tm, sub):
        rows = slice(r0, r0 + sub)
        x = x_ref[rows, :]
        u = x * lax.rsqrt(jnp.mean(x * x, axis=-1, keepdims=True) + EPS) * nw_ref[...]
        ub = u.astype(BF16)
        dt_ref[rows, :] = _softplus(_dot(ub, wdt_ref[...]) + dtb_ref[...])
        for c0 in range(XBC0, D_MAIN, tn):
            cc = slice(c0 - XBC0, c0 - XBC0 + tn)
            ext_scr[8:8 + sub, cc] = _dot(ub, w_ref[:, c0:c0 + tn])
            ea = ext_scr[:, cc]
            e1 = pltpu.roll(ea, 1, axis=0)
            pa = cw_ref[3:4, cc] * ea + cw_ref[2:3, cc] * e1
            pb = cw_ref[1:2, cc] * ea + cw_ref[0:1, cc] * e1
            conv = cb_ref[:, cc] + pa[8:8 + sub] + pb[6:6 + sub]
            proj_ref[rows, c0:c0 + tn] = _silu(conv).astype(BF16)
            ext_scr[0:8, cc] = ext_scr[sub:sub + 8, cc]
        for c0 in list(range(GATE0, XBC0, tn)) + list(range(0, GATE0, tn)):
            acc = _dot(ub, w_ref[:, c0:c0 + tn])
            if c0 < SB_WIDTH:
                acc = acc * Q_SCALE
            elif c0 >= GATE0:
                acc = _silu(acc)
            proj_ref[rows, c0:c0 + tn] = acc.astype(BF16)
    if emit_tail:
        tail_ref[...] = ext_scr[0:8, :]


def _inproj(x2d, norm_w, w_all, w_dt, dt_bias, tail0, conv_w, conv_b, *, tm, rows_per_seq, emit_tail, tn=512):
    rows = x2d.shape[0]
    sub = min(tm, 256)
    assert SB_WIDTH % tn == 0 and GATE0 % tn == 0 and XBC0 % tn == 0 and D_MAIN % tn == 0
    assert rows_per_seq % tm == 0 and tm % sub == 0
    const = lambda i: (0, 0)
    out_specs = [pl.BlockSpec((tm, D_MAIN), lambda i: (i, 0)), pl.BlockSpec((tm, LANES), lambda i: (i, 0))]
    out_shape = [jax.ShapeDtypeStruct((rows, D_MAIN), BF16), jax.ShapeDtypeStruct((rows, LANES), F32)]
    if emit_tail:
        out_specs.append(pl.BlockSpec((8, SSD_XBC), const))
        out_shape.append(jax.ShapeDtypeStruct((8, SSD_XBC), F32))
    return pl.pallas_call(
        functools.partial(_inproj_kernel, tm=tm, sub=sub, tn=tn, tiles_per_seq=rows_per_seq // tm,
                          emit_tail=emit_tail),
        grid=(rows // tm,),
        in_specs=[
            pl.BlockSpec((tm, D_MODEL), lambda i: (i, 0)),
            pl.BlockSpec((1, D_MODEL), const),
            pl.BlockSpec((D_MODEL, D_MAIN), const, pipeline_mode=pl.Buffered(1)),
            pl.BlockSpec((D_MODEL, LANES), const),
            pl.BlockSpec((1, LANES), const),
            pl.BlockSpec((8, SSD_XBC), const),
            pl.BlockSpec((SSD_CONV, SSD_XBC), const),
            pl.BlockSpec((1, SSD_XBC), const),
        ],
        out_specs=out_specs,
        out_shape=out_shape,
        scratch_shapes=[pltpu.VMEM((sub + 8, SSD_XBC), F32)],
        compiler_params=pltpu.CompilerParams(dimension_semantics=("arbitrary",)),
        name="inproj",
    )(x2d, norm_w, w_all, w_dt, dt_bias, tail0, conv_w, conv_b)


def _ssd_kernel(zact_ref, xs_ref, b_ref, c_ref, dt_ref, s0_ref, alog_ref, dsk_ref, nw_ref, e_ref,
                *rest, ts, emit_state):
    if emit_state:
        y_ref, sfin_ref, s_scr, y_scr = rest
    else:
        y_ref, s_scr, y_scr = rest

    @pl.when(pl.program_id(1) == 0)
    def _():
        s_scr[...] = s0_ref[...]

    nchunk = ts // CHUNK
    a_row = -jnp.exp(alog_ref[...])
    li = lax.broadcasted_iota(jnp.int32, (ts, ts), 0)
    si = lax.broadcasted_iota(jnp.int32, (ts, ts), 1)
    t_incl = jnp.logical_and(si <= li, si // CHUNK == li // CHUNK).astype(BF16)
    l2 = lax.broadcasted_iota(jnp.int32, (CHUNK, LANES), 0)
    lane2 = lax.broadcasted_iota(jnp.int32, (CHUNK, LANES), 1)
    mask2 = (lane2 % CHUNK) <= l2
    low_half = lane2 < HEAD_DIM
    e_mat = e_ref[...]

    xs = xs_ref[...].astype(F32)
    dt = dt_ref[...]
    h3 = _split3(dt * a_row)
    acum = _dot(t_incl, h3[0]) + _dot(t_incl, h3[1]) + _dot(t_incl, h3[2])
    st3 = _split3(jnp.concatenate([dt, acum], axis=0))
    e2 = jnp.concatenate([e_mat, e_mat], axis=0)
    rep = _dot(jnp.concatenate([st3[0], st3[1]], axis=1), e2)
    dtr = rep[0:ts]
    acr = rep[ts:2 * ts] + _dot(st3[2][ts:2 * ts], e_mat)
    xdt = xs * dtr
    ebase = jnp.exp(acr)

    chunks = [slice(c * CHUNK, (c + 1) * CHUNK) for c in range(nchunk)]

    yoffs = []
    for c, r in enumerate(chunks):
        atot = acr[(c + 1) * CHUNK - 1:(c + 1) * CHUNK, :]
        xw = (xdt[r] * jnp.exp(atot - acr[r])).astype(BF16)
        etot = jnp.exp(atot)
        for g in range(SSD_GROUPS):
            gs = slice(g * 512, (g + 1) * 512)
            ns = slice(g * SSD_STATE, (g + 1) * SSD_STATE)
            sg = s_scr[g]
            yoffs.append(_dot(c_ref[r, ns], sg.astype(BF16)) * ebase[r, gs])
            s_scr[g] = sg * etot[:, gs] + _dot_tn(b_ref[r, ns], xw[:, gs])

    wts = [jnp.concatenate([acum[r], pltpu.roll(acum[r], LANES - 1, axis=1)], axis=0).T for r in chunks]
    cb2s = []
    for r in chunks:
        for g in range(SSD_GROUPS):
            ns = slice(g * SSD_STATE, (g + 1) * SSD_STATE)
            bg = b_ref[r, ns]
            cb2s.append(_dot_nt(c_ref[r, ns], jnp.concatenate([bg, bg], axis=0)))
    npairs = SSD_HEADS // 2
    m2s, xblks = [], []
    for c, r in enumerate(chunks):
        for p in range(npairs):
            cols = slice(p * PAIR, (p + 1) * PAIR)
            seg = acr[r, cols] - wts[c][2 * p:2 * p + 1, :]
            m2s.append((cb2s[c * SSD_GROUPS + p // 4] * jnp.exp(jnp.where(mask2, seg, NEG_BIG))).astype(BF16))
            x2 = xdt[r, cols].astype(BF16)
            zero = jnp.zeros_like(x2)
            xblks.append(jnp.concatenate([jnp.where(low_half, x2, zero),
                                          jnp.where(low_half, zero, x2)], axis=0))
    ydiags = [_dot(m2, xblk) for m2, xblk in zip(m2s, xblks)]
    for c, r in enumerate(chunks):
        for p in range(npairs):
            cols = slice(p * PAIR, (p + 1) * PAIR)
            yoff = yoffs[c * SSD_GROUPS + p // 4]
            y_scr[r, cols] = (ydiags[c * npairs + p] + yoff[:, (p % 4) * PAIR:(p % 4 + 1) * PAIR]
                              + xs[r, cols] * dsk_ref[:, cols])

    yg = y_scr[...] * zact_ref[...].astype(F32)
    y_ref[...] = (yg * lax.rsqrt(jnp.mean(yg * yg, axis=-1, keepdims=True) + EPS)
                  * nw_ref[...]).astype(BF16)
    if emit_state:
        sfin_ref[...] = s_scr[...]


def _ssd(proj, dt, s0, a_log, dsk_rep, norm_w, e_mat, *, nbatch, ts, emit_state):
    rows = proj.shape[0]
    nsteps = rows // (nbatch * ts)
    rb = lambda b, s: b * nsteps + s
    const = lambda b, s: (0, 0)
    in_specs = [
        pl.BlockSpec((ts, 1024), lambda b, s: (rb(b, s), COL_Z)),
        pl.BlockSpec((ts, 1024), lambda b, s: (rb(b, s), COL_XS)),
        pl.BlockSpec((ts, 256), lambda b, s: (rb(b, s), COL_B)),
        pl.BlockSpec((ts, 256), lambda b, s: (rb(b, s), COL_C)),
        pl.BlockSpec((ts, LANES), lambda b, s: (rb(b, s), 0)),
        pl.BlockSpec((SSD_GROUPS, SSD_STATE, 512), lambda b, s: (0, 0, 0)),
        pl.BlockSpec((1, LANES), const),
        pl.BlockSpec((1, SSD_WIDTH), const),
        pl.BlockSpec((1, SSD_WIDTH), const),
        pl.BlockSpec((LANES, SSD_WIDTH), const),
    ]
    out_specs = [pl.BlockSpec((ts, SSD_WIDTH), lambda b, s: (rb(b, s), 0))]
    out_shape = [jax.ShapeDtypeStruct((rows, SSD_WIDTH), BF16)]
    if emit_state:
        out_specs.append(pl.BlockSpec((SSD_GROUPS, SSD_STATE, 512), lambda b, s: (0, 0, 0)))
        out_shape.append(jax.ShapeDtypeStruct((SSD_GROUPS, SSD_STATE, 512), F32))
    return pl.pallas_call(
        functools.partial(_ssd_kernel, ts=ts, emit_state=emit_state),
        grid=(nbatch, nsteps),
        in_specs=in_specs,
        out_specs=out_specs,
        out_shape=out_shape,
        scratch_shapes=[
            pltpu.VMEM((SSD_GROUPS, SSD_STATE, 512), F32),
            pltpu.VMEM((ts, SSD_WIDTH), F32),
        ],
        compiler_params=pltpu.CompilerParams(dimension_semantics=("arbitrary", "arbitrary")),
        name="ssd_meta" if emit_state else "ssd",
    )(proj, proj, proj, proj, dt, s0, a_log, dsk_rep, norm_w, e_mat)


def _sb_tiles(items, tmat):
    n = len(items)
    ys = [_dot_nt(it[0], it[1]).astype(BF16) for it in items]
    sps, owns = [], []
    for y, it in zip(ys, items):
        lg = jnp.log(1.0 + jnp.exp(-jnp.abs(y)))
        sp = jnp.maximum(y, 0.0) + lg
        owns.append(jnp.minimum(y, 0.0) - lg)
        if it[3] is not None:
            sp = jnp.where(it[3], sp, jnp.zeros_like(sp))
        sps.append(sp)
    excls = [_dot(sp, tmat[0:sp.shape[1], 0:sp.shape[1]]) for sp in sps]
    rss = [excls[i][:, 0:1] + sps[i][:, 0:1].astype(F32) for i in range(n)]
    c_in, c_out = [], []
    for i, it in enumerate(items):
        c = c_out[it[6]] if it[6] is not None else it[4]
        c_in.append(c)
        c_out.append(rss[i] if c is None else c + rss[i])
    ws = []
    for i, it in enumerate(items):
        t = excls[i] if c_in[i] is None else excls[i] + c_in[i]
        w = jnp.exp(owns[i] - t.astype(BF16))
        if it[3] is not None:
            w = jnp.where(it[3], w, jnp.zeros_like(w))
        ws.append(w)
    pvs = [_dot(ws[i], items[i][2]) for i in range(n)]
    a_out = []
    for i, it in enumerate(items):
        a = a_out[it[6]] if it[6] is not None else it[5]
        a_out.append(pvs[i] if a is None else a + pvs[i])
    return c_out, a_out


def _attn_kernel(q_ref, k_ref, v_ref, km_ref, vm_ref, o_ref, c_scr, acc_scr, *, nq, npair):
    nchain = 2 * npair
    HALF = TQ // 2
    lane = lax.broadcasted_iota(jnp.int32, (TQ, PAIR), 1)
    low = lane < HEAD_DIM
    row = lax.broadcasted_iota(jnp.int32, (TQ, TQ), 0)
    col = lax.broadcasted_iota(jnp.int32, (TQ, TQ), 1)
    tmat = (row > col).astype(BF16)
    diag_mask = col < row
    meta_mask = lax.broadcasted_iota(jnp.int32, (TQ, LANES), 1) < N_META

    def cols(p):
        return slice(p * PAIR, (p + 1) * PAIR)

    def load_q(r0):
        qs = []
        for p in range(npair):
            q2 = q_ref[pl.ds(r0, TQ), cols(p)]
            zero = jnp.zeros_like(q2)
            qs += [jnp.where(low, q2, zero), jnp.where(low, zero, q2)]
        return qs

    def kv_tile(qs, k0, mask, state):
        items = []
        for i in range(nchain):
            kb = k_ref[pl.ds(k0, TQ), cols(i // 2)]
            vb = v_ref[pl.ds(k0, TQ), cols(i // 2)]
            if state == "scratch":
                items.append((qs[i], kb, vb, mask, c_scr[i], acc_scr[i], None))
            elif state == "fresh":
                items.append((qs[i], kb, vb, mask, None, None, None))
            else:
                items.append((qs[i], kb, vb, mask, None, None, state + i))
        return items

    halves = (slice(0, HALF), slice(HALF, TQ))

    def diag_items(qs, r0):
        items = []
        for h, rws in enumerate(halves):
            nk = HALF if h == 0 else TQ
            for i in range(nchain):
                items.append((qs[i][rws], k_ref[pl.ds(r0, nk), cols(i // 2)],
                              v_ref[pl.ds(r0, nk), cols(i // 2)], diag_mask[rws, 0:nk], None, None, None))
        return items

    def commit(cs, accs):
        for i in range(nchain):
            c_scr[i] = cs[i]
            acc_scr[i] = accs[i]

    def commit_halves(cs, accs):
        for h, rws in enumerate(halves):
            for i in range(nchain):
                c_scr[i, rws, :] = cs[h * nchain + i]
                acc_scr[i, rws, :] = accs[h * nchain + i]

    def unfinished(rws=slice(0, TQ)):
        m = c_scr[0, rws, :]
        for i in range(1, nchain):
            m = jnp.minimum(m, c_scr[i, rws, :])
        return (jnp.min(m) < C_SKIP).astype(jnp.int32)

    def meta_tile(qs):
        items = [(qs[i], km_ref[:, cols(i // 2)], vm_ref[:, cols(i // 2)], meta_mask,
                  c_scr[i], acc_scr[i], None) for i in range(nchain)]
        commit(*_sb_tiles(items, tmat))

    def write_out(r0):
        for p in range(npair):
            o_ref[pl.ds(r0, TQ), cols(p)] = jnp.where(low, acc_scr[2 * p], acc_scr[2 * p + 1]).astype(BF16)

    qs0 = load_q(0)
    commit_halves(*_sb_tiles(diag_items(qs0, 0), tmat))
    meta_tile(qs0)
    write_out(0)

    def qblock(qi, carry):
        r0 = pl.multiple_of(qi * TQ, TQ)
        qs = load_q(r0)
        kp = pl.multiple_of(r0 - TQ, TQ)
        kmid = pl.multiple_of(r0 - HALF, HALF)
        top, bot = halves
        items = diag_items(qs, r0)
        for i in range(nchain):
            items.append((qs[i][top], k_ref[pl.ds(kp, TQ), cols(i // 2)],
                          v_ref[pl.ds(kp, TQ), cols(i // 2)], None, None, None, i))
        for i in range(nchain):
            items.append((qs[i][bot], k_ref[pl.ds(kmid, HALF), cols(i // 2)],
                          v_ref[pl.ds(kmid, HALF), cols(i // 2)], None, None, None, nchain + i))
        cs, accs = _sb_tiles(items, tmat)
        commit_halves(cs[2 * nchain:], accs[2 * nchain:])

        @pl.when(unfinished(bot) > 0)
        def _():
            its = [(qs[i][bot], k_ref[pl.ds(kp, HALF), cols(i // 2)], v_ref[pl.ds(kp, HALF), cols(i // 2)],
                    None, c_scr[i, bot, :], acc_scr[i, bot, :], None) for i in range(nchain)]
            cs2, accs2 = _sb_tiles(its, tmat)
            for i in range(nchain):
                c_scr[i, bot, :] = cs2[i]
                acc_scr[i, bot, :] = accs2[i]

        def cond(st):
            return jnp.logical_and(st[0] < qi, st[1] > 0)

        def body(st):
            k0 = pl.multiple_of((qi - 1 - st[0]) * TQ, TQ)
            commit(*_sb_tiles(kv_tile(qs, k0, None, "scratch"), tmat))
            return st[0] + 1, unfinished()

        _, more = lax.while_loop(cond, body, (jnp.int32(1), unfinished()))

        @pl.when(more > 0)
        def _():
            meta_tile(qs)

        write_out(r0)
        return carry

    lax.fori_loop(1, nq, qblock, 0)


def _attn(proj, kmeta, vmeta, *, nbatch, seq, npair=4):
    width = npair * PAIR
    nblk = SB_WIDTH // width
    return pl.pallas_call(
        functools.partial(_attn_kernel, nq=seq // TQ, npair=npair),
        grid=(nbatch, nblk),
        in_specs=[
            pl.BlockSpec((seq, width), lambda b, p: (b, COL_Q * nblk + p)),
            pl.BlockSpec((seq, width), lambda b, p: (b, COL_K * nblk + p)),
            pl.BlockSpec((seq, width), lambda b, p: (b, COL_V * nblk + p)),
            pl.BlockSpec((LANES, width), lambda b, p: (0, p)),
            pl.BlockSpec((LANES, width), lambda b, p: (0, p)),
        ],
        out_specs=pl.BlockSpec((seq, width), lambda b, p: (b, p)),
        out_shape=jax.ShapeDtypeStruct((nbatch * seq, SB_WIDTH), BF16),
        scratch_shapes=[pltpu.VMEM((2 * npair, TQ, 1), F32), pltpu.VMEM((2 * npair, TQ, PAIR), F32)],
        compiler_params=pltpu.CompilerParams(dimension_semantics=("arbitrary", "arbitrary")),
        name="sb_attn",
    )(proj, proj, proj, kmeta, vmeta)


def _outproj_kernel(x_ref, osb_ref, gate_ref, yssd_ref, sbw_ref, wo_ref, fw_ref, o_ref, *, tm, sub):
    nsub = tm // sub

    def gated_norm(s):
        rows = slice(s * sub, (s + 1) * sub)
        ys = osb_ref[rows, :].astype(F32) * gate_ref[rows, :].astype(F32)
        return (ys * lax.rsqrt(jnp.mean(ys * ys, axis=-1, keepdims=True) + EPS) * sbw_ref[...]).astype(BF16)

    ysb = gated_norm(0)
    for s in range(nsub):
        rows = slice(s * sub, (s + 1) * sub)
        acc = (_dot(ysb, wo_ref[0:SB_WIDTH, :])
               + _dot(yssd_ref[rows, :], wo_ref[SB_WIDTH:SB_WIDTH + SSD_WIDTH, :]))
        if s + 1 < nsub:
            ysb = gated_norm(s + 1)
        h = x_ref[rows, :] + acc
        o_ref[rows, :] = h * lax.rsqrt(jnp.mean(h * h, axis=-1, keepdims=True) + EPS) * fw_ref[...]


def _outproj(x2d, osb, proj, yssd, sb_norm_w, w_out, final_w, *, tm=1024):
    rows = x2d.shape[0]
    return pl.pallas_call(
        functools.partial(_outproj_kernel, tm=tm, sub=256),
        grid=(rows // tm,),
        in_specs=[
            pl.BlockSpec((tm, D_MODEL), lambda i: (i, 0)),
            pl.BlockSpec((tm, SB_WIDTH), lambda i: (i, 0)),
            pl.BlockSpec((tm, SB_WIDTH), lambda i: (i, COL_GATE)),
            pl.BlockSpec((tm, SSD_WIDTH), lambda i: (i, 0)),
            pl.BlockSpec((1, SB_WIDTH), lambda i: (0, 0)),
            pl.BlockSpec((SB_WIDTH + SSD_WIDTH, D_MODEL), lambda i: (0, 0)),
            pl.BlockSpec((1, D_MODEL), lambda i: (0, 0)),
        ],
        out_specs=pl.BlockSpec((tm, D_MODEL), lambda i: (i, 0)),
        out_shape=jax.ShapeDtypeStruct((rows, D_MODEL), F32),
        compiler_params=pltpu.CompilerParams(dimension_semantics=("arbitrary",)),
        name="outproj",
    )(x2d, osb, proj, yssd, sb_norm_w, w_out, final_w)


def _layer(x2d, meta, norm_w, w_in, conv_w, conv_b, dt_bias, a_log, d_skip, sb_norm_w,
           ssd_norm_w, w_out, nbatch, seq):
    nh = SSD_HEADS
    w_main = w_in.astype(BF16)
    w_dt = jnp.pad(w_in[:, D_MAIN:], ((0, 0), (0, LANES - nh))).astype(BF16)
    dtb = jnp.pad(dt_bias, (0, LANES - nh)).reshape(1, LANES)
    alog = jnp.pad(a_log, (0, LANES - nh)).reshape(1, LANES)
    norm_w = norm_w.reshape(1, D_MODEL)
    dsk_rep = jnp.repeat(d_skip, HEAD_DIM).reshape(1, SSD_WIDTH)
    e_mat = (jnp.arange(LANES)[:, None] == (jnp.arange(SSD_WIDTH)[None, :] // HEAD_DIM)).astype(BF16)
    conv_b = conv_b.reshape(1, SSD_XBC)
    ssd_norm_w = ssd_norm_w.reshape(1, SSD_WIDTH)

    zeros_t = jnp.zeros((8, SSD_XBC), F32)
    proj_m, dt_m, tail_meta = _inproj(meta, norm_w, w_main, w_dt, dtb, zeros_t, conv_w, conv_b,
                                      tm=N_META, rows_per_seq=N_META, emit_tail=True)
    pad = CHUNK - N_META
    proj_mp = jnp.pad(proj_m, ((pad, 0), (0, 0)))
    dt_mp = jnp.pad(dt_m, ((pad, 0), (0, 0)))
    zeros_s = jnp.zeros((SSD_GROUPS, SSD_STATE, 512), F32)
    _, s_meta = _ssd(proj_mp, dt_mp, zeros_s, alog, dsk_rep, ssd_norm_w, e_mat,
                     nbatch=1, ts=CHUNK, emit_state=True)
    kmeta = jnp.pad(proj_m[:, COL_K * 1024:(COL_K + 1) * 1024], ((0, LANES - N_META), (0, 0)))
    vmeta = jnp.pad(proj_m[:, COL_V * 1024:(COL_V + 1) * 1024], ((0, LANES - N_META), (0, 0)))

    proj, dt = _inproj(x2d, norm_w, w_main, w_dt, dtb, tail_meta, conv_w, conv_b,
                       tm=512, rows_per_seq=seq, emit_tail=False)
    (yssd,) = _ssd(proj, dt, s_meta, alog, dsk_rep, ssd_norm_w, e_mat,
                   nbatch=nbatch, ts=TQ, emit_state=False)
    osb = _attn(proj, kmeta, vmeta, nbatch=nbatch, seq=seq)
    return osb, proj, yssd


def kernel(x, meta_tokens, norm_w, w_in, conv_w, conv_b, dt_bias, a_log, d_skip, sb_norm_w,
           ssd_norm_w, w_out, final_norm_w):
    nbatch, seq, _ = x.shape
    assert norm_w.shape[0] == 1, "single-layer block"
    x2d = x.reshape(nbatch * seq, D_MODEL)
    osb, proj, yssd = _layer(x2d, meta_tokens, norm_w[0], w_in[0], conv_w[0], conv_b[0], dt_bias[0],
                             a_log[0], d_skip[0], sb_norm_w[0], ssd_norm_w[0], w_out[0], nbatch, seq)
    out = _outproj(x2d, osb, proj, yssd, sb_norm_w[0].reshape(1, SB_WIDTH), w_out[0].astype(BF16),
                   final_norm_w.reshape(1, D_MODEL))
    return out.reshape(nbatch, seq, D_MODEL)
```

```python
import functools
import math

import jax
import jax.numpy as jnp
from jax import lax
from jax.experimental import pallas as pl
from jax.experimental.pallas import tpu as pltpu

F32 = jnp.float32
BF16 = jnp.bfloat16

D_MODEL = 1024
N_META = 16
HEAD_DIM = 64
SB_WIDTH = 1024
SSD_WIDTH = 1024
SSD_HEADS = 16
SSD_GROUPS = 2
SSD_STATE = 128
SSD_CONV = 4
SSD_XBC = SSD_WIDTH + 2 * SSD_GROUPS * SSD_STATE
D_MAIN = 4 * SB_WIDTH + SSD_WIDTH + SSD_XBC
EPS = 1e-5

LANES = 128
PAIR = 2 * HEAD_DIM
CHUNK = 64
TQ = 256
NEG_BIG = -1e30
Q_SCALE = 1.0 / math.sqrt(HEAD_DIM)
C_SKIP = 111.0

COL_Q, COL_K, COL_V, COL_GATE, COL_Z, COL_XS = 0, 1, 2, 3, 4, 5
COL_B, COL_C = 24, 25
GATE0 = COL_GATE * 1024
XBC0 = COL_XS * 1024


def _softplus(x):
    return jnp.maximum(x, 0.0) + jnp.log(1.0 + jnp.exp(-jnp.abs(x)))


def _silu(x):
    h = 0.5 * x
    return h + h * jnp.tanh(h)


def _split3(x):
    hi = x.astype(BF16)
    r1 = x - hi.astype(F32)
    mid = r1.astype(BF16)
    lo = (r1 - mid.astype(F32)).astype(BF16)
    return hi, mid, lo


def _dot(a, b):
    return jnp.dot(a, b, preferred_element_type=F32)


def _dot_nt(a, b):
    return lax.dot_general(a, b, (((1,), (1,)), ((), ())), preferred_element_type=F32)


def _dot_tn(a, b):
    return lax.dot_general(a, b, (((0,), (0,)), ((), ())), preferred_element_type=F32)


def _cast_kernel(w_ref, o_ref, *, transpose):
    w = w_ref[...]
    o_ref[...] = (w.T if transpose else w).astype(BF16)


def _to_bf16(w, *, transpose=False, tr=512):
    rows, ncols = w.shape
    if transpose:
        out_spec, out_dims = pl.BlockSpec((ncols, tr), lambda i: (0, i)), (ncols, rows)
    else:
        out_spec, out_dims = pl.BlockSpec((tr, ncols), lambda i: (i, 0)), (rows, ncols)
    return pl.pallas_call(
        functools.partial(_cast_kernel, transpose=transpose),
        grid=(pl.cdiv(rows, tr),),
        in_specs=[pl.BlockSpec((tr, ncols), lambda i: (i, 0))],
        out_specs=out_spec,
        out_shape=jax.ShapeDtypeStruct(out_dims, BF16),
        compiler_params=pltpu.CompilerParams(dimension_semantics=("arbitrary",)),
        name="cast_bf16",
    )(w)


def _inproj_kernel(x_ref, nw_ref, w_ref, wdt_ref, dtb_ref, tail0_ref, cw_ref, cb_ref, *rest,
                   tm, sub, tn, tiles_per_seq, emit_tail):
    if emit_tail:
        proj_ref, dt_ref, tail_ref, ext_scr = rest
    else:
        proj_ref, dt_ref, ext_scr = rest

    @pl.when(pl.program_id(0) % tiles_per_seq == 0)
    def _():
        ext_scr[0:8, :] = tail0_ref[...]

    for r0 in range(0, tm, sub):
        rows = slice(r0, r0 + sub)
        x = x_ref[rows, :]
        u = x * lax.rsqrt(jnp.mean(x * x, axis=-1, keepdims=True) + EPS) * nw_ref[...]
        ub = u.astype(BF16)
        dt_ref[rows, :] = _softplus(_dot(ub, wdt_ref[...]) + dtb_ref[...])
        for c0 in range(XBC0, D_MAIN, tn):
            cc = slice(c0 - XBC0, c0 - XBC0 + tn)
            ext_scr[8:8 + sub, cc] = _dot(ub, w_ref[:, c0:c0 + tn])
            ea = ext_scr[:, cc]
            e1 = pltpu.roll(ea, 1, axis=0)
            pa = cw_ref[3:4, cc] * ea + cw_ref[2:3, cc] * e1
            pb = cw_ref[1:2, cc] * ea + cw_ref[0:1, cc] * e1
            conv = cb_ref[:, cc] + pa[8:8 + sub] + pb[6:6 + sub]
            proj_ref[rows, c0:c0 + tn] = _silu(conv).astype(BF16)
            ext_scr[0:8, cc] = ext_scr[sub:sub + 8, cc]
        for c0 in list(range(GATE0, XBC0, tn)) + list(range(0, GATE0, tn)):
            acc = _dot(ub, w_ref[:, c0:c0 + tn])
            if c0 < SB_WIDTH:
                acc = acc * Q_SCALE
            elif c0 >= GATE0:
                acc = _silu(acc)
            proj_ref[rows, c0:c0 + tn] = acc.astype(BF16)
    if emit_tail:
        tail_ref[...] = ext_scr[0:8, :]


def _inproj(x2d, norm_w, w_all, w_dt, dt_bias, tail0, conv_w, conv_b, *, tm, rows_per_seq, emit_tail, tn=512):
    rows = x2d.shape[0]
    sub = min(tm, 256)
    assert SB_WIDTH % tn == 0 and GATE0 % tn == 0 and XBC0 % tn == 0 and D_MAIN % tn == 0
    assert rows_per_seq % tm == 0 and tm % sub == 0
    const = lambda i: (0, 0)
    out_specs = [pl.BlockSpec((tm, D_MAIN), lambda i: (i, 0)), pl.BlockSpec((tm, LANES), lambda i: (i, 0))]
    out_shape = [jax.ShapeDtypeStruct((rows, D_MAIN), BF16), jax.ShapeDtypeStruct((rows, LANES), F32)]
    if emit_tail:
        out_specs.append(pl.BlockSpec((8, SSD_XBC), const))
        out_shape.append(jax.ShapeDtypeStruct((8, SSD_XBC), F32))
    return pl.pallas_call(
        functools.partial(_inproj_kernel, tm=tm, sub=sub, tn=tn, tiles_per_seq=rows_per_seq // tm,
                          emit_tail=emit_tail),
        grid=(rows // tm,),
        in_specs=[
            pl.BlockSpec((tm, D_MODEL), lambda i: (i, 0)),
            pl.BlockSpec((1, D_MODEL), const),
            pl.BlockSpec((D_MODEL, D_MAIN), const, pipeline_mode=pl.Buffered(1)),
            pl.BlockSpec((D_MODEL, LANES), const),
            pl.BlockSpec((1, LANES), const),
            pl.BlockSpec((8, SSD_XBC), const),
            pl.BlockSpec((SSD_CONV, SSD_XBC), const),
            pl.BlockSpec((1, SSD_XBC), const),
        ],
        out_specs=out_specs,
        out_shape=out_shape,
        scratch_shapes=[pltpu.VMEM((sub + 8, SSD_XBC), F32)],
        compiler_params=pltpu.CompilerParams(dimension_semantics=("arbitrary",)),
        name="inproj",
    )(x2d, norm_w, w_all, w_dt, dt_bias, tail0, conv_w, conv_b)


def _ssd_kernel(zact_ref, xs_ref, b_ref, c_ref, dt_ref, s0_ref, alog_ref, dsk_ref, nw_ref, e_ref,
                *rest, ts, emit_state):
    if emit_state:
        y_ref, sfin_ref, s_scr, y_scr = rest
    else:
        y_ref, s_scr, y_scr = rest

    @pl.when(pl.program_id(1) == 0)
    def _():
        s_scr[...] = s0_ref[...]

    nchunk = ts // CHUNK
    a_row = -jnp.exp(alog_ref[...])
    li = lax.broadcasted_iota(jnp.int32, (ts, ts), 0)
    si = lax.broadcasted_iota(jnp.int32, (ts, ts), 1)
    t_incl = jnp.logical_and(si <= li, si // CHUNK == li // CHUNK).astype(BF16)
    l2 = lax.broadcasted_iota(jnp.int32, (CHUNK, LANES), 0)
    lane2 = lax.broadcasted_iota(jnp.int32, (CHUNK, LANES), 1)
    mask2 = (lane2 % CHUNK) <= l2
    low_half = lane2 < HEAD_DIM
    e_mat = e_ref[...]

    xs = xs_ref[...].astype(F32)
    dt = dt_ref[...]
    h3 = _split3(dt * a_row)
    acum = _dot(t_incl, h3[0]) + _dot(t_incl, h3[1]) + _dot(t_incl, h3[2])
    st3 = _split3(jnp.concatenate([dt, acum], axis=0))
    e2 = jnp.concatenate([e_mat, e_mat], axis=0)
    rep = _dot(jnp.concatenate([st3[0], st3[1]], axis=1), e2)
    dtr = rep[0:ts]
    acr = rep[ts:2 * ts] + _dot(st3[2][ts:2 * ts], e_mat)
    xdt = xs * dtr
    ebase = jnp.exp(acr)

    chunks = [slice(c * CHUNK, (c + 1) * CHUNK) for c in range(nchunk)]

    yoffs = []
    for c, r in enumerate(chunks):
        atot = acr[(c + 1) * CHUNK - 1:(c + 1) * CHUNK, :]
        xw = (xdt[r] * jnp.exp(atot - acr[r])).astype(BF16)
        etot = jnp.exp(atot)
        for g in range(SSD_GROUPS):
            gs = slice(g * 512, (g + 1) * 512)
            ns = slice(g * SSD_STATE, (g + 1) * SSD_STATE)
            sg = s_scr[g]
            yoffs.append(_dot(c_ref[r, ns], sg.astype(BF16)) * ebase[r, gs])
            s_scr[g] = sg * etot[:, gs] + _dot_tn(b_ref[r, ns], xw[:, gs])

    wts = [jnp.concatenate([acum[r], pltpu.roll(acum[r], LANES - 1, axis=1)], axis=0).T for r in chunks]
    cb2s = []
    for r in chunks:
        for g in range(SSD_GROUPS):
            ns = slice(g * SSD_STATE, (g + 1) * SSD_STATE)
            bg = b_ref[r, ns]
            cb2s.append(_dot_nt(c_ref[r, ns], jnp.concatenate([bg, bg], axis=0)))
    npairs = SSD_HEADS // 2
    m2s, xblks = [], []
    for c, r in enumerate(chunks):
        for p in range(npairs):
            cols = slice(p * PAIR, (p + 1) * PAIR)
            seg = acr[r, cols] - wts[c][2 * p:2 * p + 1, :]
            m2s.append((cb2s[c * SSD_GROUPS + p // 4] * jnp.exp(jnp.where(mask2, seg, NEG_BIG))).astype(BF16))
            x2 = xdt[r, cols].astype(BF16)
            zero = jnp.zeros_like(x2)
            xblks.append(jnp.concatenate([jnp.where(low_half, x2, zero),
                                          jnp.where(low_half, zero, x2)], axis=0))
    ydiags = [_dot(m2, xblk) for m2, xblk in zip(m2s, xblks)]
    for c, r in enumerate(chunks):
        for p in range(npairs):
            cols = slice(p * PAIR, (p + 1) * PAIR)
            yoff = yoffs[c * SSD_GROUPS + p // 4]
            y_scr[r, cols] = (ydiags[c * npairs + p] + yoff[:, (p % 4) * PAIR:(p % 4 + 1) * PAIR]
                              + xs[r, cols] * dsk_ref[:, cols])

    yg = y_scr[...] * zact_ref[...].astype(F32)
    y_ref[...] = (yg * lax.rsqrt(jnp.mean(yg * yg, axis=-1, keepdims=True) + EPS)
                  * nw_ref[...]).astype(BF16)
    if emit_state:
        sfin_ref[...] = s_scr[...]


def _ssd(proj, dt, s0, a_log, dsk_rep, norm_w, e_mat, *, nbatch, ts, emit_state):
    rows = proj.shape[0]
    nsteps = rows // (nbatch * ts)
    rb = lambda b, s: b * nsteps + s
    const = lambda b, s: (0, 0)
    in_specs = [
        pl.BlockSpec((ts, 1024), lambda b, s: (rb(b, s), COL_Z)),
        pl.BlockSpec((ts, 1024), lambda b, s: (rb(b, s), COL_XS)),
        pl.BlockSpec((ts, 256), lambda b, s: (rb(b, s), COL_B)),
        pl.BlockSpec((ts, 256), lambda b, s: (rb(b, s), COL_C)),
        pl.BlockSpec((ts, LANES), lambda b, s: (rb(b, s), 0)),
        pl.BlockSpec((SSD_GROUPS, SSD_STATE, 512), lambda b, s: (0, 0, 0)),
        pl.BlockSpec((1, LANES), const),
        pl.BlockSpec((1, SSD_WIDTH), const),
        pl.BlockSpec((1, SSD_WIDTH), const),
        pl.BlockSpec((LANES, SSD_WIDTH), const),
    ]
    out_specs = [pl.BlockSpec((ts, SSD_WIDTH), lambda b, s: (rb(b, s), 0))]
    out_shape = [jax.ShapeDtypeStruct((rows, SSD_WIDTH), BF16)]
    if emit_state:
        out_specs.append(pl.BlockSpec((SSD_GROUPS, SSD_STATE, 512), lambda b, s: (0, 0, 0)))
        out_shape.append(jax.ShapeDtypeStruct((SSD_GROUPS, SSD_STATE, 512), F32))
    return pl.pallas_call(
        functools.partial(_ssd_kernel, ts=ts, emit_state=emit_state),
        grid=(nbatch, nsteps),
        in_specs=in_specs,
        out_specs=out_specs,
        out_shape=out_shape,
        scratch_shapes=[
            pltpu.VMEM((SSD_GROUPS, SSD_STATE, 512), F32),
            pltpu.VMEM((ts, SSD_WIDTH), F32),
        ],
        compiler_params=pltpu.CompilerParams(dimension_semantics=("arbitrary", "arbitrary")),
        name="ssd_meta" if emit_state else "ssd",
    )(proj, proj, proj, proj, dt, s0, a_log, dsk_rep, norm_w, e_mat)


def _sb_tiles(items, tmat):
    n = len(items)
    ys = [_dot_nt(it[0], it[1]).astype(BF16) for it in items]
    sps, owns = [], []
    for y, it in zip(ys, items):
        lg = jnp.log(1.0 + jnp.exp(-jnp.abs(y)))
        sp = jnp.maximum(y, 0.0) + lg
        owns.append(jnp.minimum(y, 0.0) - lg)
        if it[3] is not None:
            sp = jnp.where(it[3], sp, jnp.zeros_like(sp))
        sps.append(sp)
    excls = [_dot(sp, tmat[0:sp.shape[1], 0:sp.shape[1]]) for sp in sps]
    rss = [excls[i][:, 0:1] + sps[i][:, 0:1].astype(F32) for i in range(n)]
    c_in, c_out = [], []
    for i, it in enumerate(items):
        c = c_out[it[6]] if it[6] is not None else it[4]
        c_in.append(c)
        c_out.append(rss[i] if c is None else c + rss[i])
    ws = []
    for i, it in enumerate(items):
        t = excls[i] if c_in[i] is None else excls[i] + c_in[i]
        w = jnp.exp(owns[i] - t.astype(BF16))
        if it[3] is not None:
            w = jnp.where(it[3], w, jnp.zeros_like(w))
        ws.append(w)
    pvs = [_dot(ws[i], items[i][2]) for i in range(n)]
    a_out = []
    for i, it in enumerate(items):
        a = a_out[it[6]] if it[6] is not None else it[5]
        a_out.append(pvs[i] if a is None else a + pvs[i])
    return c_out, a_out


def _attn_kernel(q_ref, k_ref, v_ref, km_ref, vm_ref, o_ref, c_scr, acc_scr, *, nq, npair):
    nchain = 2 * npair
    HALF = TQ // 2
    lane = lax.broadcasted_iota(jnp.int32, (TQ, PAIR), 1)
    low = lane < HEAD_DIM
    row = lax.broadcasted_iota(jnp.int32, (TQ, TQ), 0)
    col = lax.broadcasted_iota(jnp.int32, (TQ, TQ), 1)
    tmat = (row > col).astype(BF16)
    diag_mask = col < row
    meta_mask = lax.broadcasted_iota(jnp.int32, (TQ, LANES), 1) < N_META

    def cols(p):
        return slice(p * PAIR, (p + 1) * PAIR)

    def load_q(r0):
        qs = []
        for p in range(npair):
            q2 = q_ref[pl.ds(r0, TQ), cols(p)]
            zero = jnp.zeros_like(q2)
            qs += [jnp.where(low, q2, zero), jnp.where(low, zero, q2)]
        return qs

    def kv_tile(qs, k0, mask, state):
        items = []
        for i in range(nchain):
            kb = k_ref[pl.ds(k0, TQ), cols(i // 2)]
            vb = v_ref[pl.ds(k0, TQ), cols(i // 2)]
            if state == "scratch":
                items.append((qs[i], kb, vb, mask, c_scr[i], acc_scr[i], None))
            elif state == "fresh":
                items.append((qs[i], kb, vb, mask, None, None, None))
            else:
                items.append((qs[i], kb, vb, mask, None, None, state + i))
        return items

    halves = (slice(0, HALF), slice(HALF, TQ))

    def diag_items(qs, r0):
        items = []
        for h, rws in enumerate(halves):
            nk = HALF if h == 0 else TQ
            for i in range(nchain):
                items.append((qs[i][rws], k_ref[pl.ds(r0, nk), cols(i // 2)],
                              v_ref[pl.ds(r0, nk), cols(i // 2)], diag_mask[rws, 0:nk], None, None, None))
        return items

    def commit(cs, accs):
        for i in range(nchain):
            c_scr[i] = cs[i]
            acc_scr[i] = accs[i]

    def commit_halves(cs, accs):
        for h, rws in enumerate(halves):
            for i in range(nchain):
                c_scr[i, rws, :] = cs[h * nchain + i]
                acc_scr[i, rws, :] = accs[h * nchain + i]

    def unfinished(rws=slice(0, TQ)):
        m = c_scr[0, rws, :]
        for i in range(1, nchain):
            m = jnp.minimum(m, c_scr[i, rws, :])
        return (jnp.min(m) < C_SKIP).astype(jnp.int32)

    def meta_tile(qs):
        items = [(qs[i], km_ref[:, cols(i // 2)], vm_ref[:, cols(i // 2)], meta_mask,
                  c_scr[i], acc_scr[i], None) for i in range(nchain)]
        commit(*_sb_tiles(items, tmat))

    def write_out(r0):
        for p in range(npair):
            o_ref[pl.ds(r0, TQ), cols(p)] = jnp.where(low, acc_scr[2 * p], acc_scr[2 * p + 1]).astype(BF16)

    qs0 = load_q(0)
    commit_halves(*_sb_tiles(diag_items(qs0, 0), tmat))
    meta_tile(qs0)
    write_out(0)

    def qblock(qi, carry):
        r0 = pl.multiple_of(qi * TQ, TQ)
        qs = load_q(r0)
        kp = pl.multiple_of(r0 - TQ, TQ)
        kmid = pl.multiple_of(r0 - HALF, HALF)
        top, bot = halves
        items = diag_items(qs, r0)
        for i in range(nchain):
            items.append((qs[i][top], k_ref[pl.ds(kp, TQ), cols(i // 2)],
                          v_ref[pl.ds(kp, TQ), cols(i // 2)], None, None, None, i))
        for i in range(nchain):
            items.append((qs[i][bot], k_ref[pl.ds(kmid, HALF), cols(i // 2)],
                          v_ref[pl.ds(kmid, HALF), cols(i // 2)], None, None, None, nchain + i))
        cs, accs = _sb_tiles(items, tmat)
        commit_halves(cs[2 * nchain:], accs[2 * nchain:])

        @pl.when(unfinished(bot) > 0)
        def _():
            its = [(qs[i][bot], k_ref[pl.ds(kp, HALF), cols(i // 2)], v_ref[pl.ds(kp, HALF), cols(i // 2)],
                    None, c_scr[i, bot, :], acc_scr[i, bot, :], None) for i in range(nchain)]
            cs2, accs2 = _sb_tiles(its, tmat)
            for i in range(nchain):
                c_scr[i, bot, :] = cs2[i]
                acc_scr[i, bot, :] = accs2[i]

        def cond(st):
            return jnp.logical_and(st[0] < qi, st[1] > 0)

        def body(st):
            k0 = pl.multiple_of((qi - 1 - st[0]) * TQ, TQ)
            commit(*_sb_tiles(kv_tile(qs, k0, None, "scratch"), tmat))
            return st[0] + 1, unfinished()

        _, more = lax.while_loop(cond, body, (jnp.int32(1), unfinished()))

        @pl.when(more > 0)
        def _():
            meta_tile(qs)

        write_out(r0)
        return carry

    lax.fori_loop(1, nq, qblock, 0)


def _attn(proj, kmeta, vmeta, *, nbatch, seq, npair=4):
    width = npair * PAIR
    nblk = SB_WIDTH // width
    return pl.pallas_call(
        functools.partial(_attn_kernel, nq=seq // TQ, npair=npair),
        grid=(nbatch, nblk),
        in_specs=[
            pl.BlockSpec((seq, width), lambda b, p: (b, COL_Q * nblk + p)),
            pl.BlockSpec((seq, width), lambda b, p: (b, COL_K * nblk + p)),
            pl.BlockSpec((seq, width), lambda b, p: (b, COL_V * nblk + p)),
            pl.BlockSpec((LANES, width), lambda b, p: (0, p)),
            pl.BlockSpec((LANES, width), lambda b, p: (0, p)),
        ],
        out_specs=pl.BlockSpec((seq, width), lambda b, p: (b, p)),
        out_shape=jax.ShapeDtypeStruct((nbatch * seq, SB_WIDTH), BF16),
        scratch_shapes=[pltpu.VMEM((2 * npair, TQ, 1), F32), pltpu.VMEM((2 * npair, TQ, PAIR), F32)],
        compiler_params=pltpu.CompilerParams(dimension_semantics=("arbitrary", "arbitrary")),
        name="sb_attn",
    )(proj, proj, proj, kmeta, vmeta)


def _outproj_kernel(x_ref, osb_ref, gate_ref, yssd_ref, sbw_ref, wo_ref, fw_ref, o_ref, *, tm, sub):
    nsub = tm // sub

    def gated_norm(s):
        rows = slice(s * sub, (s + 1) * sub)
        ys = osb_ref[rows, :].astype(F32) * gate_ref[rows, :].astype(F32)
        return (ys * lax.rsqrt(jnp.mean(ys * ys, axis=-1, keepdims=True) + EPS) * sbw_ref[...]).astype(BF16)

    ysb = gated_norm(0)
    for s in range(nsub):
        rows = slice(s * sub, (s + 1) * sub)
        acc = (_dot(ysb, wo_ref[0:SB_WIDTH, :])
               + _dot(yssd_ref[rows, :], wo_ref[SB_WIDTH:SB_WIDTH + SSD_WIDTH, :]))
        if s + 1 < nsub:
            ysb = gated_norm(s + 1)
        h = x_ref[rows, :] + acc
        o_ref[rows, :] = h * lax.rsqrt(jnp.mean(h * h, axis=-1, keepdims=True) + EPS) * fw_ref[...]


def _outproj(x2d, osb, proj, yssd, sb_norm_w, w_out, final_w, *, tm=1024):
    rows = x2d.shape[0]
    return pl.pallas_call(
        functools.partial(_outproj_kernel, tm=tm, sub=256),
        grid=(rows // tm,),
        in_specs=[
            pl.BlockSpec((tm, D_MODEL), lambda i: (i, 0)),
            pl.BlockSpec((tm, SB_WIDTH), lambda i: (i, 0)),
            pl.BlockSpec((tm, SB_WIDTH), lambda i: (i, COL_GATE)),
            pl.BlockSpec((tm, SSD_WIDTH), lambda i: (i, 0)),
            pl.BlockSpec((1, SB_WIDTH), lambda i: (0, 0)),
            pl.BlockSpec((SB_WIDTH + SSD_WIDTH, D_MODEL), lambda i: (0, 0)),
            pl.BlockSpec((1, D_MODEL), lambda i: (0, 0)),
        ],
        out_specs=pl.BlockSpec((tm, D_MODEL), lambda i: (i, 0)),
        out_shape=jax.ShapeDtypeStruct((rows, D_MODEL), F32),
        compiler_params=pltpu.CompilerParams(dimension_semantics=("arbitrary",)),
        name="outproj",
    )(x2d, osb, proj, yssd, sb_norm_w, w_out, final_w)


def _layer(x2d, meta, norm_w, w_in, conv_w, conv_b, dt_bias, a_log, d_skip, sb_norm_w,
           ssd_norm_w, w_out, nbatch, seq):
    nh = SSD_HEADS
    w_main = _to_bf16(jnp.swapaxes(w_in, 0, 1), transpose=True)
    w_dt = jnp.pad(w_main[:, D_MAIN:], ((0, 0), (0, LANES - nh)))
    dtb = jnp.pad(dt_bias, (0, LANES - nh)).reshape(1, LANES)
    alog = jnp.pad(a_log, (0, LANES - nh)).reshape(1, LANES)
    norm_w = norm_w.reshape(1, D_MODEL)
    dsk_rep = jnp.repeat(d_skip, HEAD_DIM).reshape(1, SSD_WIDTH)
    e_mat = (jnp.arange(LANES)[:, None] == (jnp.arange(SSD_WIDTH)[None, :] // HEAD_DIM)).astype(BF16)
    conv_b = conv_b.reshape(1, SSD_XBC)
    ssd_norm_w = ssd_norm_w.reshape(1, SSD_WIDTH)

    zeros_t = jnp.zeros((8, SSD_XBC), F32)
    proj_m, dt_m, tail_meta = _inproj(meta, norm_w, w_main, w_dt, dtb, zeros_t, conv_w, conv_b,
                                      tm=N_META, rows_per_seq=N_META, emit_tail=True)
    pad = CHUNK - N_META
    proj_mp = jnp.pad(proj_m, ((pad, 0), (0, 0)))
    dt_mp = jnp.pad(dt_m, ((pad, 0), (0, 0)))
    zeros_s = jnp.zeros((SSD_GROUPS, SSD_STATE, 512), F32)
    _, s_meta = _ssd(proj_mp, dt_mp, zeros_s, alog, dsk_rep, ssd_norm_w, e_mat,
                     nbatch=1, ts=CHUNK, emit_state=True)
    kmeta = jnp.pad(proj_m[:, COL_K * 1024:(COL_K + 1) * 1024], ((0, LANES - N_META), (0, 0)))
    vmeta = jnp.pad(proj_m[:, COL_V * 1024:(COL_V + 1) * 1024], ((0, LANES - N_META), (0, 0)))

    proj, dt = _inproj(x2d, norm_w, w_main, w_dt, dtb, tail_meta, conv_w, conv_b,
                       tm=512, rows_per_seq=seq, emit_tail=False)
    (yssd,) = _ssd(proj, dt, s_meta, alog, dsk_rep, ssd_norm_w, e_mat,
                   nbatch=nbatch, ts=TQ, emit_state=False)
    osb = _attn(proj, kmeta, vmeta, nbatch=nbatch, seq=seq)
    return osb, proj, yssd


def kernel(x, meta_tokens, norm_w, w_in, conv_w, conv_b, dt_bias, a_log, d_skip, sb_norm_w,
           ssd_norm_w, w_out, final_norm_w):
    nbatch, seq, _ = x.shape
    assert norm_w.shape[0] == 1, "single-layer block"
    x2d = x.reshape(nbatch * seq, D_MODEL)
    osb, proj, yssd = _layer(x2d, meta_tokens, norm_w[0], w_in[0], conv_w[0], conv_b[0], dt_bias[0],
                             a_log[0], d_skip[0], sb_norm_w[0], ssd_norm_w[0], w_out[0], nbatch, seq)
    out = _outproj(x2d, osb, proj, yssd, sb_norm_w[0].reshape(1, SB_WIDTH), _to_bf16(w_out[0]),
                   final_norm_w.reshape(1, D_MODEL))
    return out.reshape(nbatch, seq, D_MODEL)
```

```python
import functools
import math

import jax
import jax.numpy as jnp
from jax import lax
from jax.experimental import pallas as pl
from jax.experimental.pallas import tpu as pltpu

F32 = jnp.float32
BF16 = jnp.bfloat16

D_MODEL = 1024
N_META = 16
HEAD_DIM = 64
SB_WIDTH = 1024
SSD_WIDTH = 1024
SSD_HEADS = 16
SSD_GROUPS = 2
SSD_STATE = 128
SSD_CONV = 4
SSD_XBC = SSD_WIDTH + 2 * SSD_GROUPS * SSD_STATE
D_MAIN = 4 * SB_WIDTH + SSD_WIDTH + SSD_XBC
EPS = 1e-5

LANES = 128
PAIR = 2 * HEAD_DIM
CHUNK = 64
TQ = 256
NEG_BIG = -1e30
Q_SCALE = 1.0 / math.sqrt(HEAD_DIM)
C_SKIP = 111.0

COL_Q, COL_K, COL_V, COL_GATE, COL_Z, COL_XS = 0, 1, 2, 3, 4, 5
COL_B, COL_C = 24, 25
GATE0 = COL_GATE * 1024
XBC0 = COL_XS * 1024


def _softplus(x):
    return jnp.maximum(x, 0.0) + jnp.log(1.0 + jnp.exp(-jnp.abs(x)))


def _silu(x):
    h = 0.5 * x
    return h + h * jnp.tanh(h)


def _split3(x):
    hi = x.astype(BF16)
    r1 = x - hi.astype(F32)
    mid = r1.astype(BF16)
    lo = (r1 - mid.astype(F32)).astype(BF16)
    return hi, mid, lo


def _dot(a, b):
    return jnp.dot(a, b, preferred_element_type=F32)


def _dot_nt(a, b):
    return lax.dot_general(a, b, (((1,), (1,)), ((), ())), preferred_element_type=F32)


def _dot_tn(a, b):
    return lax.dot_general(a, b, (((0,), (0,)), ((), ())), preferred_element_type=F32)


def _cast_kernel(w_ref, o_ref, *, transpose):
    w = w_ref[...]
    o_ref[...] = (w.T if transpose else w).astype(BF16)


def _to_bf16(w, *, transpose=False, tr=512):
    rows, ncols = w.shape
    if transpose:
        out_spec, out_dims = pl.BlockSpec((ncols, tr), lambda i: (0, i)), (ncols, rows)
    else:
        out_spec, out_dims = pl.BlockSpec((tr, ncols), lambda i: (i, 0)), (rows, ncols)
    return pl.pallas_call(
        functools.partial(_cast_kernel, transpose=transpose),
        grid=(pl.cdiv(rows, tr),),
        in_specs=[pl.BlockSpec((tr, ncols), lambda i: (i, 0))],
        out_specs=out_spec,
        out_shape=jax.ShapeDtypeStruct(out_dims, BF16),
        compiler_params=pltpu.CompilerParams(dimension_semantics=("arbitrary",)),
        name="cast_bf16",
    )(w)


def _inproj_kernel(x_ref, nw_ref, w_ref, wdt_ref, dtb_ref, tail0_ref, cw_ref, cb_ref, *rest,
                   tm, sub, tn, tiles_per_seq, emit_tail):
    if emit_tail:
        proj_ref, dt_ref, tail_ref, ext_scr = rest
    else:
        proj_ref, dt_ref, ext_scr = rest

    @pl.when(pl.program_id(0) % tiles_per_seq == 0)
    def _():
        ext_scr[0:8, :] = tail0_ref[...]

    for r0 in range(0, tm, sub):
        rows = slice(r0, r0 + sub)
        x = x_ref[rows, :]
        u = x * lax.rsqrt(jnp.mean(x * x, axis=-1, keepdims=True) + EPS) * nw_ref[...]
        ub = u.astype(BF16)
        dt_ref[rows, :] = _softplus(_dot(ub, wdt_ref[...]) + dtb_ref[...])
        for c0 in range(XBC0, D_MAIN, tn):
            cc = slice(c0 - XBC0, c0 - XBC0 + tn)
            ext_scr[8:8 + sub, cc] = _dot(ub, w_ref[:, c0:c0 + tn])
            ea = ext_scr[:, cc]
            e1 = pltpu.roll(ea, 1, axis=0)
            pa = cw_ref[3:4, cc] * ea + cw_ref[2:3, cc] * e1
            pb = cw_ref[1:2, cc] * ea + cw_ref[0:1, cc] * e1
            conv = cb_ref[:, cc] + pa[8:8 + sub] + pb[6:6 + sub]
            proj_ref[rows, c0:c0 + tn] = _silu(conv).astype(BF16)
            ext_scr[0:8, cc] = ext_scr[sub:sub + 8, cc]
        for c0 in list(range(GATE0, XBC0, tn)) + list(range(0, GATE0, tn)):
            acc = _dot(ub, w_ref[:, c0:c0 + tn])
            if c0 < SB_WIDTH:
                acc = acc * Q_SCALE
            elif c0 >= GATE0:
                acc = _silu(acc)
            proj_ref[rows, c0:c0 + tn] = acc.astype(BF16)
    if emit_tail:
        tail_ref[...] = ext_scr[0:8, :]


def _inproj(x2d, norm_w, w_all, w_dt, dt_bias, tail0, conv_w, conv_b, *, tm, rows_per_seq, emit_tail, tn=512):
    rows = x2d.shape[0]
    sub = min(tm, 256)
    assert SB_WIDTH % tn == 0 and GATE0 % tn == 0 and XBC0 % tn == 0 and D_MAIN % tn == 0
    assert rows_per_seq % tm == 0 and tm % sub == 0
    const = lambda i: (0, 0)
    out_specs = [pl.BlockSpec((tm, D_MAIN), lambda i: (i, 0)), pl.BlockSpec((tm, LANES), lambda i: (i, 0))]
    out_shape = [jax.ShapeDtypeStruct((rows, D_MAIN), BF16), jax.ShapeDtypeStruct((rows, LANES), F32)]
    if emit_tail:
        out_specs.append(pl.BlockSpec((8, SSD_XBC), const))
        out_shape.append(jax.ShapeDtypeStruct((8, SSD_XBC), F32))
    return pl.pallas_call(
        functools.partial(_inproj_kernel, tm=tm, sub=sub, tn=tn, tiles_per_seq=rows_per_seq // tm,
                          emit_tail=emit_tail),
        grid=(rows // tm,),
        in_specs=[
            pl.BlockSpec((tm, D_MODEL), lambda i: (i, 0)),
            pl.BlockSpec((1, D_MODEL), const),
            pl.BlockSpec((D_MODEL, D_MAIN), const, pipeline_mode=pl.Buffered(1)),
            pl.BlockSpec((D_MODEL, LANES), const),
            pl.BlockSpec((1, LANES), const),
            pl.BlockSpec((8, SSD_XBC), const),
            pl.BlockSpec((SSD_CONV, SSD_XBC), const),
            pl.BlockSpec((1, SSD_XBC), const),
        ],
        out_specs=out_specs,
        out_shape=out_shape,
        scratch_shapes=[pltpu.VMEM((sub + 8, SSD_XBC), F32)],
        compiler_params=pltpu.CompilerParams(dimension_semantics=("arbitrary",)),
        name="inproj",
    )(x2d, norm_w, w_all, w_dt, dt_bias, tail0, conv_w, conv_b)


def _ssd_kernel(zact_ref, xs_ref, b_ref, c_ref, dt_ref, s0_ref, alog_ref, dsk_ref, nw_ref, e_ref,
                *rest, ts, emit_state):
    if emit_state:
        y_ref, sfin_ref, s_scr, y_scr = rest
    else:
        y_ref, s_scr, y_scr = rest

    @pl.when(pl.program_id(1) == 0)
    def _():
        s_scr[...] = s0_ref[...]

    nchunk = ts // CHUNK
    a_row = -jnp.exp(alog_ref[...])
    li = lax.broadcasted_iota(jnp.int32, (ts, ts), 0)
    si = lax.broadcasted_iota(jnp.int32, (ts, ts), 1)
    t_incl = jnp.logical_and(si <= li, si // CHUNK == li // CHUNK).astype(BF16)
    l2 = lax.broadcasted_iota(jnp.int32, (CHUNK, LANES), 0)
    lane2 = lax.broadcasted_iota(jnp.int32, (CHUNK, LANES), 1)
    mask2 = (lane2 % CHUNK) <= l2
    low_half = lane2 < HEAD_DIM
    e_mat = e_ref[...]

    xs = xs_ref[...].astype(F32)
    dt = dt_ref[...]
    h3 = _split3(dt * a_row)
    acum = _dot(t_incl, h3[0]) + _dot(t_incl, h3[1]) + _dot(t_incl, h3[2])
    st3 = _split3(jnp.concatenate([dt, acum], axis=0))
    e2 = jnp.concatenate([e_mat, e_mat], axis=0)
    rep = _dot(jnp.concatenate([st3[0], st3[1]], axis=1), e2)
    dtr = rep[0:ts]
    acr = rep[ts:2 * ts]
    xdt = xs * dtr
    ebase = jnp.exp(acr)

    chunks = [slice(c * CHUNK, (c + 1) * CHUNK) for c in range(nchunk)]

    yoffs = []
    for c, r in enumerate(chunks):
        atot = acr[(c + 1) * CHUNK - 1:(c + 1) * CHUNK, :]
        xw = (xdt[r] * jnp.exp(atot - acr[r])).astype(BF16)
        etot = jnp.exp(atot)
        for g in range(SSD_GROUPS):
            gs = slice(g * 512, (g + 1) * 512)
            ns = slice(g * SSD_STATE, (g + 1) * SSD_STATE)
            sg = s_scr[g]
            yoffs.append(_dot(c_ref[r, ns], sg.astype(BF16)) * ebase[r, gs])
            s_scr[g] = sg * etot[:, gs] + _dot_tn(b_ref[r, ns], xw[:, gs])

    wts = [jnp.concatenate([acum[r], pltpu.roll(acum[r], LANES - 1, axis=1)], axis=0).T for r in chunks]
    cb2s = []
    for r in chunks:
        for g in range(SSD_GROUPS):
            ns = slice(g * SSD_STATE, (g + 1) * SSD_STATE)
            bg = b_ref[r, ns]
            cb2s.append(_dot_nt(c_ref[r, ns], jnp.concatenate([bg, bg], axis=0)))
    npairs = SSD_HEADS // 2
    m2s, xblks = [], []
    for c, r in enumerate(chunks):
        for p in range(npairs):
            cols = slice(p * PAIR, (p + 1) * PAIR)
            seg = acr[r, cols] - wts[c][2 * p:2 * p + 1, :]
            m2s.append((cb2s[c * SSD_GROUPS + p // 4] * jnp.exp(jnp.where(mask2, seg, NEG_BIG))).astype(BF16))
            x2 = xdt[r, cols].astype(BF16)
            zero = jnp.zeros_like(x2)
            xblks.append(jnp.concatenate([jnp.where(low_half, x2, zero),
                                          jnp.where(low_half, zero, x2)], axis=0))
    ydiags = [_dot(m2, xblk) for m2, xblk in zip(m2s, xblks)]
    for c, r in enumerate(chunks):
        for p in range(npairs):
            cols = slice(p * PAIR, (p + 1) * PAIR)
            yoff = yoffs[c * SSD_GROUPS + p // 4]
            y_scr[r, cols] = (ydiags[c * npairs + p] + yoff[:, (p % 4) * PAIR:(p % 4 + 1) * PAIR]
                              + xs[r, cols] * dsk_ref[:, cols])

    yg = y_scr[...] * zact_ref[...].astype(F32)
    y_ref[...] = (yg * lax.rsqrt(jnp.mean(yg * yg, axis=-1, keepdims=True) + EPS)
                  * nw_ref[...]).astype(BF16)
    if emit_state:
        sfin_ref[...] = s_scr[...]


def _ssd(proj, dt, s0, a_log, dsk_rep, norm_w, e_mat, *, nbatch, ts, emit_state):
    rows = proj.shape[0]
    nsteps = rows // (nbatch * ts)
    rb = lambda b, s: b * nsteps + s
    const = lambda b, s: (0, 0)
    in_specs = [
        pl.BlockSpec((ts, 1024), lambda b, s: (rb(b, s), COL_Z)),
        pl.BlockSpec((ts, 1024), lambda b, s: (rb(b, s), COL_XS)),
        pl.BlockSpec((ts, 256), lambda b, s: (rb(b, s), COL_B)),
        pl.BlockSpec((ts, 256), lambda b, s: (rb(b, s), COL_C)),
        pl.BlockSpec((ts, LANES), lambda b, s: (rb(b, s), 0)),
        pl.BlockSpec((SSD_GROUPS, SSD_STATE, 512), lambda b, s: (0, 0, 0)),
        pl.BlockSpec((1, LANES), const),
        pl.BlockSpec((1, SSD_WIDTH), const),
        pl.BlockSpec((1, SSD_WIDTH), const),
        pl.BlockSpec((LANES, SSD_WIDTH), const),
    ]
    out_specs = [pl.BlockSpec((ts, SSD_WIDTH), lambda b, s: (rb(b, s), 0))]
    out_shape = [jax.ShapeDtypeStruct((rows, SSD_WIDTH), BF16)]
    if emit_state:
        out_specs.append(pl.BlockSpec((SSD_GROUPS, SSD_STATE, 512), lambda b, s: (0, 0, 0)))
        out_shape.append(jax.ShapeDtypeStruct((SSD_GROUPS, SSD_STATE, 512), F32))
    return pl.pallas_call(
        functools.partial(_ssd_kernel, ts=ts, emit_state=emit_state),
        grid=(nbatch, nsteps),
        in_specs=in_specs,
        out_specs=out_specs,
        out_shape=out_shape,
        scratch_shapes=[
            pltpu.VMEM((SSD_GROUPS, SSD_STATE, 512), F32),
            pltpu.VMEM((ts, SSD_WIDTH), F32),
        ],
        compiler_params=pltpu.CompilerParams(dimension_semantics=("arbitrary", "arbitrary")),
        name="ssd_meta" if emit_state else "ssd",
    )(proj, proj, proj, proj, dt, s0, a_log, dsk_rep, norm_w, e_mat)


def _sb_tiles(items, tmat):
    n = len(items)
    ys = [_dot_nt(it[0], it[1]).astype(BF16) for it in items]
    sps, owns = [], []
    for y, it in zip(ys, items):
        lg = jnp.log(1.0 + jnp.exp(-jnp.abs(y)))
        sp = jnp.maximum(y, 0.0) + lg
        owns.append(jnp.minimum(y, 0.0) - lg)
        if it[3] is not None:
            sp = jnp.where(it[3], sp, jnp.zeros_like(sp))
        sps.append(sp)
    excls = [_dot(sp, tmat[0:sp.shape[1], 0:sp.shape[1]]) for sp in sps]
    rss = [excls[i][:, 0:1] + sps[i][:, 0:1].astype(F32) for i in range(n)]
    c_in, c_out = [], []
    for i, it in enumerate(items):
        c = c_out[it[6]] if it[6] is not None else it[4]
        c_in.append(c)
        c_out.append(rss[i] if c is None else c + rss[i])
    ws = []
    for i, it in enumerate(items):
        t = excls[i] if c_in[i] is None else excls[i] + c_in[i]
        w = jnp.exp(owns[i] - t.astype(BF16))
        if it[3] is not None:
            w = jnp.where(it[3], w, jnp.zeros_like(w))
        ws.append(w)
    pvs = [_dot(ws[i], items[i][2]) for i in range(n)]
    a_out = []
    for i, it in enumerate(items):
        a = a_out[it[6]] if it[6] is not None else it[5]
        a_out.append(pvs[i] if a is None else a + pvs[i])
    return c_out, a_out


def _attn_kernel(q_ref, k_ref, v_ref, km_ref, vm_ref, o_ref, c_scr, acc_scr, *, nq, npair):
    nchain = 2 * npair
    HALF = TQ // 2
    lane = lax.broadcasted_iota(jnp.int32, (TQ, PAIR), 1)
    low = lane < HEAD_DIM
    row = lax.broadcasted_iota(jnp.int32, (TQ, TQ), 0)
    col = lax.broadcasted_iota(jnp.int32, (TQ, TQ), 1)
    tmat = (row > col).astype(BF16)
    diag_mask = col < row
    meta_mask = lax.broadcasted_iota(jnp.int32, (TQ, LANES), 1) < N_META

    def cols(p):
        return slice(p * PAIR, (p + 1) * PAIR)

    def load_q(r0):
        qs = []
        for p in range(npair):
            q2 = q_ref[pl.ds(r0, TQ), cols(p)]
            zero = jnp.zeros_like(q2)
            qs += [jnp.where(low, q2, zero), jnp.where(low, zero, q2)]
        return qs

    def kv_tile(qs, k0, mask, state):
        items = []
        for i in range(nchain):
            kb = k_ref[pl.ds(k0, TQ), cols(i // 2)]
            vb = v_ref[pl.ds(k0, TQ), cols(i // 2)]
            if state == "scratch":
                items.append((qs[i], kb, vb, mask, c_scr[i], acc_scr[i], None))
            elif state == "fresh":
                items.append((qs[i], kb, vb, mask, None, None, None))
            else:
                items.append((qs[i], kb, vb, mask, None, None, state + i))
        return items

    halves = (slice(0, HALF), slice(HALF, TQ))

    def diag_items(qs, r0):
        items = []
        for h, rws in enumerate(halves):
            nk = HALF if h == 0 else TQ
            for i in range(nchain):
                items.append((qs[i][rws], k_ref[pl.ds(r0, nk), cols(i // 2)],
                              v_ref[pl.ds(r0, nk), cols(i // 2)], diag_mask[rws, 0:nk], None, None, None))
        return items

    def commit(cs, accs):
        for i in range(nchain):
            c_scr[i] = cs[i]
            acc_scr[i] = accs[i]

    def commit_halves(cs, accs):
        for h, rws in enumerate(halves):
            for i in range(nchain):
                c_scr[i, rws, :] = cs[h * nchain + i]
                acc_scr[i, rws, :] = accs[h * nchain + i]

    def unfinished(rws=slice(0, TQ)):
        m = c_scr[0, rws, :]
        for i in range(1, nchain):
            m = jnp.minimum(m, c_scr[i, rws, :])
        return (jnp.min(m) < C_SKIP).astype(jnp.int32)

    def meta_tile(qs):
        items = [(qs[i], km_ref[:, cols(i // 2)], vm_ref[:, cols(i // 2)], meta_mask,
                  c_scr[i], acc_scr[i], None) for i in range(nchain)]
        commit(*_sb_tiles(items, tmat))

    def write_out(r0):
        for p in range(npair):
            o_ref[pl.ds(r0, TQ), cols(p)] = jnp.where(low, acc_scr[2 * p], acc_scr[2 * p + 1]).astype(BF16)

    qs0 = load_q(0)
    commit_halves(*_sb_tiles(diag_items(qs0, 0), tmat))
    meta_tile(qs0)
    write_out(0)

    def qblock(qi, carry):
        r0 = pl.multiple_of(qi * TQ, TQ)
        qs = load_q(r0)
        kp = pl.multiple_of(r0 - TQ, TQ)
        kmid = pl.multiple_of(r0 - HALF, HALF)
        top, bot = halves
        items = diag_items(qs, r0)
        for i in range(nchain):
            items.append((qs[i][top], k_ref[pl.ds(kp, TQ), cols(i // 2)],
                          v_ref[pl.ds(kp, TQ), cols(i // 2)], None, None, None, i))
        for i in range(nchain):
            items.append((qs[i][bot], k_ref[pl.ds(kmid, HALF), cols(i // 2)],
                          v_ref[pl.ds(kmid, HALF), cols(i // 2)], None, None, None, nchain + i))
        cs, accs = _sb_tiles(items, tmat)
        commit_halves(cs[2 * nchain:], accs[2 * nchain:])

        @pl.when(unfinished(bot) > 0)
        def _():
            its = [(qs[i][bot], k_ref[pl.ds(kp, HALF), cols(i // 2)], v_ref[pl.ds(kp, HALF), cols(i // 2)],
                    None, c_scr[i, bot, :], acc_scr[i, bot, :], None) for i in range(nchain)]
            cs2, accs2 = _sb_tiles(its, tmat)
            for i in range(nchain):
                c_scr[i, bot, :] = cs2[i]
                acc_scr[i, bot, :] = accs2[i]

        def cond(st):
            return jnp.logical_and(st[0] < qi, st[1] > 0)

        def body(st):
            k0 = pl.multiple_of((qi - 1 - st[0]) * TQ, TQ)
            commit(*_sb_tiles(kv_tile(qs, k0, None, "scratch"), tmat))
            return st[0] + 1, unfinished()

        _, more = lax.while_loop(cond, body, (jnp.int32(1), unfinished()))

        @pl.when(more > 0)
        def _():
            meta_tile(qs)

        write_out(r0)
        return carry

    lax.fori_loop(1, nq, qblock, 0)


def _attn(proj, kmeta, vmeta, *, nbatch, seq, npair=4):
    width = npair * PAIR
    nblk = SB_WIDTH // width
    return pl.pallas_call(
        functools.partial(_attn_kernel, nq=seq // TQ, npair=npair),
        grid=(nbatch, nblk),
        in_specs=[
            pl.BlockSpec((seq, width), lambda b, p: (b, COL_Q * nblk + p)),
            pl.BlockSpec((seq, width), lambda b, p: (b, COL_K * nblk + p)),
            pl.BlockSpec((seq, width), lambda b, p: (b, COL_V * nblk + p)),
            pl.BlockSpec((LANES, width), lambda b, p: (0, p)),
            pl.BlockSpec((LANES, width), lambda b, p: (0, p)),
        ],
        out_specs=pl.BlockSpec((seq, width), lambda b, p: (b, p)),
        out_shape=jax.ShapeDtypeStruct((nbatch * seq, SB_WIDTH), BF16),
        scratch_shapes=[pltpu.VMEM((2 * npair, TQ, 1), F32), pltpu.VMEM((2 * npair, TQ, PAIR), F32)],
        compiler_params=pltpu.CompilerParams(dimension_semantics=("arbitrary", "arbitrary")),
        name="sb_attn",
    )(proj, proj, proj, kmeta, vmeta)


def _outproj_kernel(x_ref, osb_ref, gate_ref, yssd_ref, sbw_ref, wo_ref, fw_ref, o_ref, *, tm, sub):
    nsub = tm // sub

    def gated_norm(s):
        rows = slice(s * sub, (s + 1) * sub)
        ys = osb_ref[rows, :].astype(F32) * gate_ref[rows, :].astype(F32)
        return (ys * lax.rsqrt(jnp.mean(ys * ys, axis=-1, keepdims=True) + EPS) * sbw_ref[...]).astype(BF16)

    ysb = gated_norm(0)
    for s in range(nsub):
        rows = slice(s * sub, (s + 1) * sub)
        acc = (_dot(ysb, wo_ref[0:SB_WIDTH, :])
               + _dot(yssd_ref[rows, :], wo_ref[SB_WIDTH:SB_WIDTH + SSD_WIDTH, :]))
        if s + 1 < nsub:
            ysb = gated_norm(s + 1)
        h = x_ref[rows, :] + acc
        o_ref[rows, :] = h * lax.rsqrt(jnp.mean(h * h, axis=-1, keepdims=True) + EPS) * fw_ref[...]


def _outproj(x2d, osb, proj, yssd, sb_norm_w, w_out, final_w, *, tm=1024):
    rows = x2d.shape[0]
    return pl.pallas_call(
        functools.partial(_outproj_kernel, tm=tm, sub=256),
        grid=(rows // tm,),
        in_specs=[
            pl.BlockSpec((tm, D_MODEL), lambda i: (i, 0)),
            pl.BlockSpec((tm, SB_WIDTH), lambda i: (i, 0)),
            pl.BlockSpec((tm, SB_WIDTH), lambda i: (i, COL_GATE)),
            pl.BlockSpec((tm, SSD_WIDTH), lambda i: (i, 0)),
            pl.BlockSpec((1, SB_WIDTH), lambda i: (0, 0)),
            pl.BlockSpec((SB_WIDTH + SSD_WIDTH, D_MODEL), lambda i: (0, 0)),
            pl.BlockSpec((1, D_MODEL), lambda i: (0, 0)),
        ],
        out_specs=pl.BlockSpec((tm, D_MODEL), lambda i: (i, 0)),
        out_shape=jax.ShapeDtypeStruct((rows, D_MODEL), F32),
        compiler_params=pltpu.CompilerParams(dimension_semantics=("arbitrary",)),
        name="outproj",
    )(x2d, osb, proj, yssd, sb_norm_w, w_out, final_w)


def _layer(x2d, meta, norm_w, w_in, conv_w, conv_b, dt_bias, a_log, d_skip, sb_norm_w,
           ssd_norm_w, w_out, nbatch, seq):
    nh = SSD_HEADS
    w_main = _to_bf16(jnp.swapaxes(w_in, 0, 1), transpose=True)
    w_dt = jnp.pad(w_main[:, D_MAIN:], ((0, 0), (0, LANES - nh)))
    dtb = jnp.pad(dt_bias, (0, LANES - nh)).reshape(1, LANES)
    alog = jnp.pad(a_log, (0, LANES - nh)).reshape(1, LANES)
    norm_w = norm_w.reshape(1, D_MODEL)
    dsk_rep = jnp.repeat(d_skip, HEAD_DIM).reshape(1, SSD_WIDTH)
    e_mat = (jnp.arange(LANES)[:, None] == (jnp.arange(SSD_WIDTH)[None, :] // HEAD_DIM)).astype(BF16)
    conv_b = conv_b.reshape(1, SSD_XBC)
    ssd_norm_w = ssd_norm_w.reshape(1, SSD_WIDTH)

    zeros_t = jnp.zeros((8, SSD_XBC), F32)
    proj_m, dt_m, tail_meta = _inproj(meta, norm_w, w_main, w_dt, dtb, zeros_t, conv_w, conv_b,
                                      tm=N_META, rows_per_seq=N_META, emit_tail=True)
    pad = CHUNK - N_META
    proj_mp = jnp.pad(proj_m, ((pad, 0), (0, 0)))
    dt_mp = jnp.pad(dt_m, ((pad, 0), (0, 0)))
    zeros_s = jnp.zeros((SSD_GROUPS, SSD_STATE, 512), F32)
    _, s_meta = _ssd(proj_mp, dt_mp, zeros_s, alog, dsk_rep, ssd_norm_w, e_mat,
                     nbatch=1, ts=CHUNK, emit_state=True)
    kmeta = jnp.pad(proj_m[:, COL_K * 1024:(COL_K + 1) * 1024], ((0, LANES - N_META), (0, 0)))
    vmeta = jnp.pad(proj_m[:, COL_V * 1024:(COL_V + 1) * 1024], ((0, LANES - N_META), (0, 0)))

    proj, dt = _inproj(x2d, norm_w, w_main, w_dt, dtb, tail_meta, conv_w, conv_b,
                       tm=512, rows_per_seq=seq, emit_tail=False)
    (yssd,) = _ssd(proj, dt, s_meta, alog, dsk_rep, ssd_norm_w, e_mat,
                   nbatch=nbatch, ts=TQ, emit_state=False)
    osb = _attn(proj, kmeta, vmeta, nbatch=nbatch, seq=seq)
    return osb, proj, yssd


def kernel(x, meta_tokens, norm_w, w_in, conv_w, conv_b, dt_bias, a_log, d_skip, sb_norm_w,
           ssd_norm_w, w_out, final_norm_w):
    nbatch, seq, _ = x.shape
    assert norm_w.shape[0] == 1, "single-layer block"
    x2d = x.reshape(nbatch * seq, D_MODEL)
    osb, proj, yssd = _layer(x2d, meta_tokens, norm_w[0], w_in[0], conv_w[0], conv_b[0], dt_bias[0],
                             a_log[0], d_skip[0], sb_norm_w[0], ssd_norm_w[0], w_out[0], nbatch, seq)
    out = _outproj(x2d, osb, proj, yssd, sb_norm_w[0].reshape(1, SB_WIDTH), _to_bf16(w_out[0]),
                   final_norm_w.reshape(1, D_MODEL))
    return out.reshape(nbatch, seq, D_MODEL)
```

```python
import functools
import math

import jax
import jax.numpy as jnp
from jax import lax
from jax.experimental import pallas as pl
from jax.experimental.pallas import tpu as pltpu

F32 = jnp.float32
BF16 = jnp.bfloat16

D_MODEL = 1024
N_META = 16
HEAD_DIM = 64
SB_WIDTH = 1024
SSD_WIDTH = 1024
SSD_HEADS = 16
SSD_GROUPS = 2
SSD_STATE = 128
SSD_CONV = 4
SSD_XBC = SSD_WIDTH + 2 * SSD_GROUPS * SSD_STATE
D_MAIN = 4 * SB_WIDTH + SSD_WIDTH + SSD_XBC
EPS = 1e-5

LANES = 128
PAIR = 2 * HEAD_DIM
CHUNK = 64
TQ = 256
NEG_BIG = -1e30
Q_SCALE = 1.0 / math.sqrt(HEAD_DIM)
C_SKIP = 111.0

COL_Q, COL_K, COL_V, COL_GATE, COL_Z, COL_XS = 0, 1, 2, 3, 4, 5
COL_B, COL_C = 24, 25
GATE0 = COL_GATE * 1024
XBC0 = COL_XS * 1024


def _softplus(x):
    return jnp.maximum(x, 0.0) + jnp.log(1.0 + jnp.exp(-jnp.abs(x)))


def _silu(x):
    h = 0.5 * x
    return h + h * jnp.tanh(h)


def _split3(x):
    hi = x.astype(BF16)
    r1 = x - hi.astype(F32)
    mid = r1.astype(BF16)
    lo = (r1 - mid.astype(F32)).astype(BF16)
    return hi, mid, lo


def _dot(a, b):
    return jnp.dot(a, b, preferred_element_type=F32)


def _dot_nt(a, b):
    return lax.dot_general(a, b, (((1,), (1,)), ((), ())), preferred_element_type=F32)


def _dot_tn(a, b):
    return lax.dot_general(a, b, (((0,), (0,)), ((), ())), preferred_element_type=F32)


def _cast_kernel(w_ref, o_ref, *, transpose):
    w = w_ref[...]
    o_ref[...] = (w.T if transpose else w).astype(BF16)


def _to_bf16(w, *, transpose=False, tr=512):
    rows, ncols = w.shape
    if transpose:
        out_spec, out_dims = pl.BlockSpec((ncols, tr), lambda i: (0, i)), (ncols, rows)
    else:
        out_spec, out_dims = pl.BlockSpec((tr, ncols), lambda i: (i, 0)), (rows, ncols)
    return pl.pallas_call(
        functools.partial(_cast_kernel, transpose=transpose),
        grid=(pl.cdiv(rows, tr),),
        in_specs=[pl.BlockSpec((tr, ncols), lambda i: (i, 0))],
        out_specs=out_spec,
        out_shape=jax.ShapeDtypeStruct(out_dims, BF16),
        compiler_params=pltpu.CompilerParams(dimension_semantics=("arbitrary",)),
        name="cast_bf16",
    )(w)


def _inproj_kernel(x_ref, nw_ref, w_ref, wdt_ref, dtb_ref, tail0_ref, cw_ref, cb_ref, *rest,
                   tm, sub, tn, tiles_per_seq, emit_tail):
    if emit_tail:
        proj_ref, dt_ref, tail_ref, ext_scr = rest
    else:
        proj_ref, dt_ref, ext_scr = rest

    @pl.when(pl.program_id(0) % tiles_per_seq == 0)
    def _():
        ext_scr[0:8, :] = tail0_ref[...]

    for r0 in range(0, tm, sub):
        rows = slice(r0, r0 + sub)
        x = x_ref[rows, :]
        u = x * lax.rsqrt(jnp.mean(x * x, axis=-1, keepdims=True) + EPS) * nw_ref[...]
        ub = u.astype(BF16)
        dt_ref[rows, :] = _softplus(_dot(ub, wdt_ref[...]) + dtb_ref[...])
        for c0 in range(XBC0, D_MAIN, tn):
            cc = slice(c0 - XBC0, c0 - XBC0 + tn)
            ext_scr[8:8 + sub, cc] = _dot(ub, w_ref[:, c0:c0 + tn])
            ea = ext_scr[:, cc]
            e1 = pltpu.roll(ea, 1, axis=0)
            pa = cw_ref[3:4, cc] * ea + cw_ref[2:3, cc] * e1
            pb = cw_ref[1:2, cc] * ea + cw_ref[0:1, cc] * e1
            conv = cb_ref[:, cc] + pa[8:8 + sub] + pb[6:6 + sub]
            proj_ref[rows, c0:c0 + tn] = _silu(conv).astype(BF16)
            ext_scr[0:8, cc] = ext_scr[sub:sub + 8, cc]
        for c0 in list(range(GATE0, XBC0, tn)) + list(range(0, GATE0, tn)):
            acc = _dot(ub, w_ref[:, c0:c0 + tn])
            if c0 < SB_WIDTH:
                acc = acc * Q_SCALE
            elif c0 >= GATE0:
                acc = _silu(acc)
            proj_ref[rows, c0:c0 + tn] = acc.astype(BF16)
    if emit_tail:
        tail_ref[...] = ext_scr[0:8, :]


def _inproj(x2d, norm_w, w_all, w_dt, dt_bias, tail0, conv_w, conv_b, *, tm, rows_per_seq, emit_tail, tn=512):
    rows = x2d.shape[0]
    sub = min(tm, 256)
    assert SB_WIDTH % tn == 0 and GATE0 % tn == 0 and XBC0 % tn == 0 and D_MAIN % tn == 0
    assert rows_per_seq % tm == 0 and tm % sub == 0
    const = lambda i: (0, 0)
    out_specs = [pl.BlockSpec((tm, D_MAIN), lambda i: (i, 0)), pl.BlockSpec((tm, LANES), lambda i: (i, 0))]
    out_shape = [jax.ShapeDtypeStruct((rows, D_MAIN), BF16), jax.ShapeDtypeStruct((rows, LANES), F32)]
    if emit_tail:
        out_specs.append(pl.BlockSpec((8, SSD_XBC), const))
        out_shape.append(jax.ShapeDtypeStruct((8, SSD_XBC), F32))
    return pl.pallas_call(
        functools.partial(_inproj_kernel, tm=tm, sub=sub, tn=tn, tiles_per_seq=rows_per_seq // tm,
                          emit_tail=emit_tail),
        grid=(rows // tm,),
        in_specs=[
            pl.BlockSpec((tm, D_MODEL), lambda i: (i, 0)),
            pl.BlockSpec((1, D_MODEL), const),
            pl.BlockSpec((D_MODEL, D_MAIN), const, pipeline_mode=pl.Buffered(1)),
            pl.BlockSpec((D_MODEL, LANES), const),
            pl.BlockSpec((1, LANES), const),
            pl.BlockSpec((8, SSD_XBC), const),
            pl.BlockSpec((SSD_CONV, SSD_XBC), const),
            pl.BlockSpec((1, SSD_XBC), const),
        ],
        out_specs=out_specs,
        out_shape=out_shape,
        scratch_shapes=[pltpu.VMEM((sub + 8, SSD_XBC), F32)],
        compiler_params=pltpu.CompilerParams(dimension_semantics=("arbitrary",)),
        name="inproj",
    )(x2d, norm_w, w_all, w_dt, dt_bias, tail0, conv_w, conv_b)


def _ssd_kernel(zact_ref, xs_ref, b_ref, c_ref, dt_ref, s0_ref, alog_ref, dsk_ref, nw_ref, e_ref,
                *rest, ts, emit_state):
    if emit_state:
        y_ref, sfin_ref, s_scr, y_scr = rest
    else:
        y_ref, s_scr, y_scr = rest

    @pl.when(pl.program_id(1) == 0)
    def _():
        s_scr[...] = s0_ref[...]

    nchunk = ts // CHUNK
    a_row = -jnp.exp(alog_ref[...])
    li = lax.broadcasted_iota(jnp.int32, (ts, ts), 0)
    si = lax.broadcasted_iota(jnp.int32, (ts, ts), 1)
    t_incl = jnp.logical_and(si <= li, si // CHUNK == li // CHUNK).astype(BF16)
    l2 = lax.broadcasted_iota(jnp.int32, (CHUNK, LANES), 0)
    lane2 = lax.broadcasted_iota(jnp.int32, (CHUNK, LANES), 1)
    mask2 = (lane2 % CHUNK) <= l2
    low_half = lane2 < HEAD_DIM
    e_mat = e_ref[...]

    xs = xs_ref[...].astype(F32)
    dt = dt_ref[...]
    h3 = _split3(dt * a_row)
    acum = _dot(t_incl, h3[0]) + _dot(t_incl, h3[1]) + _dot(t_incl, h3[2])
    st3 = _split3(jnp.concatenate([dt, acum], axis=0))
    e2 = jnp.concatenate([e_mat, e_mat], axis=0)
    rep = _dot(jnp.concatenate([st3[0], st3[1]], axis=1), e2)
    dtr = rep[0:ts]
    acr = rep[ts:2 * ts]
    xdt = xs * dtr
    ebase = jnp.exp(acr)

    chunks = [slice(c * CHUNK, (c + 1) * CHUNK) for c in range(nchunk)]

    yoffs = []
    for c, r in enumerate(chunks):
        atot = acr[(c + 1) * CHUNK - 1:(c + 1) * CHUNK, :]
        xw = (xdt[r] * jnp.exp(atot - acr[r])).astype(BF16)
        etot = jnp.exp(atot)
        for g in range(SSD_GROUPS):
            gs = slice(g * 512, (g + 1) * 512)
            ns = slice(g * SSD_STATE, (g + 1) * SSD_STATE)
            sg = s_scr[g]
            yoffs.append(_dot(c_ref[r, ns], sg.astype(BF16)) * ebase[r, gs])
            s_scr[g] = sg * etot[:, gs] + _dot_tn(b_ref[r, ns], xw[:, gs])

    wts = [jnp.concatenate([acum[r], pltpu.roll(acum[r], LANES - 1, axis=1)], axis=0).T for r in chunks]
    cb2s = []
    for r in chunks:
        for g in range(SSD_GROUPS):
            ns = slice(g * SSD_STATE, (g + 1) * SSD_STATE)
            bg = b_ref[r, ns]
            cb2s.append(_dot_nt(c_ref[r, ns], jnp.concatenate([bg, bg], axis=0)))
    npairs = SSD_HEADS // 2
    m2s, xblks = [], []
    for c, r in enumerate(chunks):
        for p in range(npairs):
            cols = slice(p * PAIR, (p + 1) * PAIR)
            seg = acr[r, cols] - wts[c][2 * p:2 * p + 1, :]
            m2s.append((cb2s[c * SSD_GROUPS + p // 4] * jnp.exp(jnp.where(mask2, seg, NEG_BIG))).astype(BF16))
            x2 = xdt[r, cols].astype(BF16)
            zero = jnp.zeros_like(x2)
            xblks.append(jnp.concatenate([jnp.where(low_half, x2, zero),
                                          jnp.where(low_half, zero, x2)], axis=0))
    ydiags = [_dot(m2, xblk) for m2, xblk in zip(m2s, xblks)]
    for c, r in enumerate(chunks):
        for p in range(npairs):
            cols = slice(p * PAIR, (p + 1) * PAIR)
            yoff = yoffs[c * SSD_GROUPS + p // 4]
            y_scr[r, cols] = (ydiags[c * npairs + p] + yoff[:, (p % 4) * PAIR:(p % 4 + 1) * PAIR]
                              + xs[r, cols] * dsk_ref[:, cols])

    yg = y_scr[...] * zact_ref[...].astype(F32)
    y_ref[...] = (yg * lax.rsqrt(jnp.mean(yg * yg, axis=-1, keepdims=True) + EPS)
                  * nw_ref[...]).astype(BF16)
    if emit_state:
        sfin_ref[...] = s_scr[...]


def _ssd(proj, dt, s0, a_log, dsk_rep, norm_w, e_mat, *, nbatch, ts, emit_state):
    rows = proj.shape[0]
    nsteps = rows // (nbatch * ts)
    rb = lambda b, s: b * nsteps + s
    const = lambda b, s: (0, 0)
    in_specs = [
        pl.BlockSpec((ts, 1024), lambda b, s: (rb(b, s), COL_Z)),
        pl.BlockSpec((ts, 1024), lambda b, s: (rb(b, s), COL_XS)),
        pl.BlockSpec((ts, 256), lambda b, s: (rb(b, s), COL_B)),
        pl.BlockSpec((ts, 256), lambda b, s: (rb(b, s), COL_C)),
        pl.BlockSpec((ts, LANES), lambda b, s: (rb(b, s), 0)),
        pl.BlockSpec((SSD_GROUPS, SSD_STATE, 512), lambda b, s: (0, 0, 0)),
        pl.BlockSpec((1, LANES), const),
        pl.BlockSpec((1, SSD_WIDTH), const),
        pl.BlockSpec((1, SSD_WIDTH), const),
        pl.BlockSpec((LANES, SSD_WIDTH), const),
    ]
    out_specs = [pl.BlockSpec((ts, SSD_WIDTH), lambda b, s: (rb(b, s), 0))]
    out_shape = [jax.ShapeDtypeStruct((rows, SSD_WIDTH), BF16)]
    if emit_state:
        out_specs.append(pl.BlockSpec((SSD_GROUPS, SSD_STATE, 512), lambda b, s: (0, 0, 0)))
        out_shape.append(jax.ShapeDtypeStruct((SSD_GROUPS, SSD_STATE, 512), F32))
    return pl.pallas_call(
        functools.partial(_ssd_kernel, ts=ts, emit_state=emit_state),
        grid=(nbatch, nsteps),
        in_specs=in_specs,
        out_specs=out_specs,
        out_shape=out_shape,
        scratch_shapes=[
            pltpu.VMEM((SSD_GROUPS, SSD_STATE, 512), F32),
            pltpu.VMEM((ts, SSD_WIDTH), F32),
        ],
        compiler_params=pltpu.CompilerParams(dimension_semantics=("arbitrary", "arbitrary")),
        name="ssd_meta" if emit_state else "ssd",
    )(proj, proj, proj, proj, dt, s0, a_log, dsk_rep, norm_w, e_mat)


def _paired(dot, lhs, rhs):
    outs = []
    for j in range(0, len(lhs), 2):
        rows = lhs[j].shape[0]
        both = dot(jnp.concatenate([lhs[j], lhs[j + 1]], axis=0), rhs[j])
        outs += [both[0:rows], both[rows:2 * rows]]
    return outs


def _sb_tiles(items, tmat):
    n = len(items)
    ys = [y.astype(BF16) for y in _paired(_dot_nt, [it[0] for it in items], [it[1] for it in items])]
    sps, owns = [], []
    for y, it in zip(ys, items):
        lg = jnp.log(1.0 + jnp.exp(-jnp.abs(y)))
        sp = jnp.maximum(y, 0.0) + lg
        owns.append(jnp.minimum(y, 0.0) - lg)
        if it[3] is not None:
            sp = jnp.where(it[3], sp, jnp.zeros_like(sp))
        sps.append(sp)
    excls = _paired(_dot, sps, [tmat[0:sp.shape[1], 0:sp.shape[1]] for sp in sps])
    rss = [excls[i][:, 0:1] + sps[i][:, 0:1].astype(F32) for i in range(n)]
    c_in, c_out = [], []
    for i, it in enumerate(items):
        c = c_out[it[6]] if it[6] is not None else it[4]
        c_in.append(c)
        c_out.append(rss[i] if c is None else c + rss[i])
    ws = []
    for i, it in enumerate(items):
        t = excls[i] if c_in[i] is None else excls[i] + c_in[i]
        w = jnp.exp(owns[i] - t.astype(BF16))
        if it[3] is not None:
            w = jnp.where(it[3], w, jnp.zeros_like(w))
        ws.append(w)
    pvs = _paired(_dot, ws, [it[2] for it in items])
    a_out = []
    for i, it in enumerate(items):
        a = a_out[it[6]] if it[6] is not None else it[5]
        a_out.append(pvs[i] if a is None else a + pvs[i])
    return c_out, a_out


def _attn_kernel(q_ref, k_ref, v_ref, km_ref, vm_ref, o_ref, c_scr, acc_scr, *, nq, npair):
    nchain = 2 * npair
    HALF = TQ // 2
    lane = lax.broadcasted_iota(jnp.int32, (TQ, PAIR), 1)
    low = lane < HEAD_DIM
    row = lax.broadcasted_iota(jnp.int32, (TQ, TQ), 0)
    col = lax.broadcasted_iota(jnp.int32, (TQ, TQ), 1)
    tmat = (row > col).astype(BF16)
    diag_mask = col < row
    meta_mask = lax.broadcasted_iota(jnp.int32, (TQ, LANES), 1) < N_META

    def cols(p):
        return slice(p * PAIR, (p + 1) * PAIR)

    def load_q(r0):
        qs = []
        for p in range(npair):
            q2 = q_ref[pl.ds(r0, TQ), cols(p)]
            zero = jnp.zeros_like(q2)
            qs += [jnp.where(low, q2, zero), jnp.where(low, zero, q2)]
        return qs

    def kv_tile(qs, k0, mask, state):
        items = []
        for i in range(nchain):
            kb = k_ref[pl.ds(k0, TQ), cols(i // 2)]
            vb = v_ref[pl.ds(k0, TQ), cols(i // 2)]
            if state == "scratch":
                items.append((qs[i], kb, vb, mask, c_scr[i], acc_scr[i], None))
            elif state == "fresh":
                items.append((qs[i], kb, vb, mask, None, None, None))
            else:
                items.append((qs[i], kb, vb, mask, None, None, state + i))
        return items

    halves = (slice(0, HALF), slice(HALF, TQ))

    def diag_items(qs, r0):
        items = []
        for h, rws in enumerate(halves):
            nk = HALF if h == 0 else TQ
            for i in range(nchain):
                items.append((qs[i][rws], k_ref[pl.ds(r0, nk), cols(i // 2)],
                              v_ref[pl.ds(r0, nk), cols(i // 2)], diag_mask[rws, 0:nk], None, None, None))
        return items

    def commit(cs, accs):
        for i in range(nchain):
            c_scr[i] = cs[i]
            acc_scr[i] = accs[i]

    def commit_halves(cs, accs):
        for h, rws in enumerate(halves):
            for i in range(nchain):
                c_scr[i, rws, :] = cs[h * nchain + i]
                acc_scr[i, rws, :] = accs[h * nchain + i]

    def unfinished(rws=slice(0, TQ)):
        m = c_scr[0, rws, :]
        for i in range(1, nchain):
            m = jnp.minimum(m, c_scr[i, rws, :])
        return (jnp.min(m) < C_SKIP).astype(jnp.int32)

    def meta_tile(qs):
        items = [(qs[i], km_ref[:, cols(i // 2)], vm_ref[:, cols(i // 2)], meta_mask,
                  c_scr[i], acc_scr[i], None) for i in range(nchain)]
        commit(*_sb_tiles(items, tmat))

    def write_out(r0):
        for p in range(npair):
            o_ref[pl.ds(r0, TQ), cols(p)] = jnp.where(low, acc_scr[2 * p], acc_scr[2 * p + 1]).astype(BF16)

    qs0 = load_q(0)
    commit_halves(*_sb_tiles(diag_items(qs0, 0), tmat))
    meta_tile(qs0)
    write_out(0)

    def qblock(qi, carry):
        r0 = pl.multiple_of(qi * TQ, TQ)
        qs = load_q(r0)
        kp = pl.multiple_of(r0 - TQ, TQ)
        kmid = pl.multiple_of(r0 - HALF, HALF)
        top, bot = halves
        items = diag_items(qs, r0)
        for i in range(nchain):
            items.append((qs[i][top], k_ref[pl.ds(kp, TQ), cols(i // 2)],
                          v_ref[pl.ds(kp, TQ), cols(i // 2)], None, None, None, i))
        for i in range(nchain):
            items.append((qs[i][bot], k_ref[pl.ds(kmid, HALF), cols(i // 2)],
                          v_ref[pl.ds(kmid, HALF), cols(i // 2)], None, None, None, nchain + i))
        cs, accs = _sb_tiles(items, tmat)
        commit_halves(cs[2 * nchain:], accs[2 * nchain:])

        @pl.when(unfinished(bot) > 0)
        def _():
            its = [(qs[i][bot], k_ref[pl.ds(kp, HALF), cols(i // 2)], v_ref[pl.ds(kp, HALF), cols(i // 2)],
                    None, c_scr[i, bot, :], acc_scr[i, bot, :], None) for i in range(nchain)]
            cs2, accs2 = _sb_tiles(its, tmat)
            for i in range(nchain):
                c_scr[i, bot, :] = cs2[i]
                acc_scr[i, bot, :] = accs2[i]

        def cond(st):
            return jnp.logical_and(st[0] < qi, st[1] > 0)

        def body(st):
            k0 = pl.multiple_of((qi - 1 - st[0]) * TQ, TQ)
            commit(*_sb_tiles(kv_tile(qs, k0, None, "scratch"), tmat))
            return st[0] + 1, unfinished()

        _, more = lax.while_loop(cond, body, (jnp.int32(1), unfinished()))

        @pl.when(more > 0)
        def _():
            meta_tile(qs)

        write_out(r0)
        return carry

    lax.fori_loop(1, nq, qblock, 0)


def _attn(proj, kmeta, vmeta, *, nbatch, seq, npair=4):
    width = npair * PAIR
    nblk = SB_WIDTH // width
    return pl.pallas_call(
        functools.partial(_attn_kernel, nq=seq // TQ, npair=npair),
        grid=(nbatch, nblk),
        in_specs=[
            pl.BlockSpec((seq, width), lambda b, p: (b, COL_Q * nblk + p)),
            pl.BlockSpec((seq, width), lambda b, p: (b, COL_K * nblk + p)),
            pl.BlockSpec((seq, width), lambda b, p: (b, COL_V * nblk + p)),
            pl.BlockSpec((LANES, width), lambda b, p: (0, p)),
            pl.BlockSpec((LANES, width), lambda b, p: (0, p)),
        ],
        out_specs=pl.BlockSpec((seq, width), lambda b, p: (b, p)),
        out_shape=jax.ShapeDtypeStruct((nbatch * seq, SB_WIDTH), BF16),
        scratch_shapes=[pltpu.VMEM((2 * npair, TQ, 1), F32), pltpu.VMEM((2 * npair, TQ, PAIR), F32)],
        compiler_params=pltpu.CompilerParams(dimension_semantics=("arbitrary", "arbitrary")),
        name="sb_attn",
    )(proj, proj, proj, kmeta, vmeta)


def _outproj_kernel(x_ref, osb_ref, gate_ref, yssd_ref, sbw_ref, wo_ref, fw_ref, o_ref, *, tm, sub):
    nsub = tm // sub

    def gated_norm(s):
        rows = slice(s * sub, (s + 1) * sub)
        ys = osb_ref[rows, :].astype(F32) * gate_ref[rows, :].astype(F32)
        return (ys * lax.rsqrt(jnp.mean(ys * ys, axis=-1, keepdims=True) + EPS) * sbw_ref[...]).astype(BF16)

    ysb = gated_norm(0)
    for s in range(nsub):
        rows = slice(s * sub, (s + 1) * sub)
        acc = (_dot(ysb, wo_ref[0:SB_WIDTH, :])
               + _dot(yssd_ref[rows, :], wo_ref[SB_WIDTH:SB_WIDTH + SSD_WIDTH, :]))
        if s + 1 < nsub:
            ysb = gated_norm(s + 1)
        h = x_ref[rows, :] + acc
        o_ref[rows, :] = h * lax.rsqrt(jnp.mean(h * h, axis=-1, keepdims=True) + EPS) * fw_ref[...]


def _outproj(x2d, osb, proj, yssd, sb_norm_w, w_out, final_w, *, tm=1024):
    rows = x2d.shape[0]
    return pl.pallas_call(
        functools.partial(_outproj_kernel, tm=tm, sub=256),
        grid=(rows // tm,),
        in_specs=[
            pl.BlockSpec((tm, D_MODEL), lambda i: (i, 0)),
            pl.BlockSpec((tm, SB_WIDTH), lambda i: (i, 0)),
            pl.BlockSpec((tm, SB_WIDTH), lambda i: (i, COL_GATE)),
            pl.BlockSpec((tm, SSD_WIDTH), lambda i: (i, 0)),
            pl.BlockSpec((1, SB_WIDTH), lambda i: (0, 0)),
            pl.BlockSpec((SB_WIDTH + SSD_WIDTH, D_MODEL), lambda i: (0, 0)),
            pl.BlockSpec((1, D_MODEL), lambda i: (0, 0)),
        ],
        out_specs=pl.BlockSpec((tm, D_MODEL), lambda i: (i, 0)),
        out_shape=jax.ShapeDtypeStruct((rows, D_MODEL), F32),
        compiler_params=pltpu.CompilerParams(dimension_semantics=("arbitrary",)),
        name="outproj",
    )(x2d, osb, proj, yssd, sb_norm_w, w_out, final_w)


def _layer(x2d, meta, norm_w, w_in, conv_w, conv_b, dt_bias, a_log, d_skip, sb_norm_w,
           ssd_norm_w, w_out, nbatch, seq):
    nh = SSD_HEADS
    w_main = _to_bf16(jnp.swapaxes(w_in, 0, 1), transpose=True)
    w_dt = jnp.pad(w_main[:, D_MAIN:], ((0, 0), (0, LANES - nh)))
    dtb = jnp.pad(dt_bias, (0, LANES - nh)).reshape(1, LANES)
    alog = jnp.pad(a_log, (0, LANES - nh)).reshape(1, LANES)
    norm_w = norm_w.reshape(1, D_MODEL)
    dsk_rep = jnp.repeat(d_skip, HEAD_DIM).reshape(1, SSD_WIDTH)
    e_mat = (jnp.arange(LANES)[:, None] == (jnp.arange(SSD_WIDTH)[None, :] // HEAD_DIM)).astype(BF16)
    conv_b = conv_b.reshape(1, SSD_XBC)
    ssd_norm_w = ssd_norm_w.reshape(1, SSD_WIDTH)

    zeros_t = jnp.zeros((8, SSD_XBC), F32)
    proj_m, dt_m, tail_meta = _inproj(meta, norm_w, w_main, w_dt, dtb, zeros_t, conv_w, conv_b,
                                      tm=N_META, rows_per_seq=N_META, emit_tail=True)
    pad = CHUNK - N_META
    proj_mp = jnp.pad(proj_m, ((pad, 0), (0, 0)))
    dt_mp = jnp.pad(dt_m, ((pad, 0), (0, 0)))
    zeros_s = jnp.zeros((SSD_GROUPS, SSD_STATE, 512), F32)
    _, s_meta = _ssd(proj_mp, dt_mp, zeros_s, alog, dsk_rep, ssd_norm_w, e_mat,
                     nbatch=1, ts=CHUNK, emit_state=True)
    kmeta = jnp.pad(proj_m[:, COL_K * 1024:(COL_K + 1) * 1024], ((0, LANES - N_META), (0, 0)))
    vmeta = jnp.pad(proj_m[:, COL_V * 1024:(COL_V + 1) * 1024], ((0, LANES - N_META), (0, 0)))

    proj, dt = _inproj(x2d, norm_w, w_main, w_dt, dtb, tail_meta, conv_w, conv_b,
                       tm=512, rows_per_seq=seq, emit_tail=False)
    (yssd,) = _ssd(proj, dt, s_meta, alog, dsk_rep, ssd_norm_w, e_mat,
                   nbatch=nbatch, ts=TQ, emit_state=False)
    osb = _attn(proj, kmeta, vmeta, nbatch=nbatch, seq=seq)
    return osb, proj, yssd


def kernel(x, meta_tokens, norm_w, w_in, conv_w, conv_b, dt_bias, a_log, d_skip, sb_norm_w,
           ssd_norm_w, w_out, final_norm_w):
    nbatch, seq, _ = x.shape
    assert norm_w.shape[0] == 1, "single-layer block"
    x2d = x.reshape(nbatch * seq, D_MODEL)
    osb, proj, yssd = _layer(x2d, meta_tokens, norm_w[0], w_in[0], conv_w[0], conv_b[0], dt_bias[0],
                             a_log[0], d_skip[0], sb_norm_w[0], ssd_norm_w[0], w_out[0], nbatch, seq)
    out = _outproj(x2d, osb, proj, yssd, sb_norm_w[0].reshape(1, SB_WIDTH), _to_bf16(w_out[0]),
                   final_norm_w.reshape(1, D_MODEL))
    return out.reshape(nbatch, seq, D_MODEL)
```
